```python
import jax, jax.numpy as jnp
from jax import lax
import numpy as np

D_MODEL = 2048
BATCH = 2
SEQ = 16384
DEPTH = 4
DEC_BATCH = 32
DEC_SEQ = 16
PAST_LEN = 2048

CHUNK = 64
D_RNN = 1024
N_LRU_BLOCKS = 8
LRU_BLOCK = D_RNN // N_LRU_BLOCKS
CONV_WIDTH = 4
LRU_C = 8.0
N_HEADS = 8
HEAD_DIM = 128
D_ATTN = N_HEADS * HEAD_DIM
D_FF = 4 * D_MODEL
D_PLE = 256
Q_BLOCK = 128
EPS = 1e-6
IN_SIZES = (D_RNN, D_RNN, D_ATTN, D_ATTN, D_ATTN, N_HEADS, D_MODEL, D_MODEL)
N_IN = 2 * D_RNN + 3 * D_ATTN + N_HEADS + 2 * D_MODEL

kernel_name = 'forgetting_rglru_hybrid_step'

F32 = jnp.float32


def rmsnorm(x, g):
    x32 = x.astype(F32)
    y = x32 * lax.rsqrt(jnp.mean(x32 * x32, axis=-1, keepdims=True) + EPS)
    return (y * g.astype(F32)).astype(x.dtype)


def split_in(z):
    bounds = np.cumsum(IN_SIZES)[:-1].tolist()
    return jnp.split(z, bounds, axis=-1)


def causal_conv(xa, prev, w, b):
    T = xa.shape[1]
    xp = jnp.concatenate([prev.astype(xa.dtype), xa], axis=1)
    y = b.astype(xa.dtype)
    for j in range(CONV_WIDTH):
        y = y + w[j] * xp[:, j:j + T]
    return y, xp[:, -(CONV_WIDTH - 1):]


def rg_lru(xc, h0, w_r, b_r, w_i, b_i, lam):
    B, T, _ = xc.shape
    x32 = xc.astype(F32)
    xb = x32.reshape(B, T, N_LRU_BLOCKS, LRU_BLOCK)
    r = jax.nn.sigmoid(jnp.einsum('btnc,ncd->btnd', xb, w_r.astype(F32)).reshape(B, T, D_RNN) + b_r.astype(F32))
    i = jax.nn.sigmoid(jnp.einsum('btnc,ncd->btnd', xb, w_i.astype(F32)).reshape(B, T, D_RNN) + b_i.astype(F32))
    log_a = -LRU_C * r * jax.nn.softplus(-lam.astype(F32))
    a = jnp.exp(log_a)
    u = jnp.sqrt(-jnp.expm1(2.0 * log_a)) * (i * x32)
    u = u.at[:, 0].add(a[:, 0] * h0.astype(F32))

    def combine(left, right):
        a1, b1 = left
        a2, b2 = right
        return a1 * a2, a2 * b1 + b2

    _, h = lax.associative_scan(combine, (a, u), axis=1)
    return h, h[:, -1]


def fox_prompt(q, k, v, logf):
    B, H, S, _ = q.shape
    c = jnp.cumsum(logf, axis=-1)
    nb = S // Q_BLOCK
    qb = q.reshape(B, H, nb, Q_BLOCK, HEAD_DIM).transpose(2, 0, 1, 3, 4)
    cb = c.reshape(B, H, nb, Q_BLOCK).transpose(2, 0, 1, 3)
    pos_k = jnp.arange(S)
    pos_qb = pos_k.reshape(nb, Q_BLOCK)
    scale = HEAD_DIM ** -0.5

    def block(args):
        qi, ci, pi = args
        s = jnp.einsum('bhqd,bhkd->bhqk', qi, k, preferred_element_type=F32) * scale
        s = s + (ci[..., :, None] - c[:, :, None, :])
        s = jnp.where(pos_k[None, :] <= pi[:, None], s, -jnp.inf)
        p = jax.nn.softmax(s, axis=-1)
        return jnp.einsum('bhqk,bhkd->bhqd', p.astype(v.dtype), v)

    o = lax.map(block, (qb, cb, pos_qb))
    return o.transpose(1, 2, 0, 3, 4).reshape(B, H, S, HEAD_DIM)


def fox_sample(q, k, v, logf, k_past, v_past, logf_past):
    scale = HEAD_DIM ** -0.5
    T = q.shape[2]
    P = k_past.shape[2]
    c_past = jnp.cumsum(logf_past.astype(F32), axis=-1)
    c_new = c_past[..., -1:] + jnp.cumsum(logf, axis=-1)
    s_p = jnp.einsum('bhtd,bhpd->bhtp', q, k_past, preferred_element_type=F32) * scale
    s_p = s_p + (c_new[..., :, None] - c_past[..., None, :])
    s_n = jnp.einsum('bhtd,bhsd->bhts', q, k, preferred_element_type=F32) * scale
    s_n = s_n + (c_new[..., :, None] - c_new[..., None, :])
    causal = jnp.arange(T)[None, :] <= jnp.arange(T)[:, None]
    s_n = jnp.where(causal, s_n, -jnp.inf)
    p = jax.nn.softmax(jnp.concatenate([s_p, s_n], axis=-1), axis=-1)
    o = jnp.einsum('bhtp,bhpd->bhtd', p[..., :P].astype(v_past.dtype), v_past)
    o = o + jnp.einsum('bhts,bhsd->bhtd', p[..., P:].astype(v.dtype), v)
    return o.astype(q.dtype)


def trunk_layer(x, p_i, conv_prev, h0, kv_past, lw):
    (norm_mix, norm_mlp, norm_ple, w_in, b_f, conv_w, conv_b, w_rg, b_rg, w_ig, b_ig,
     lru_lambda, q_gain, k_gain, w_a_out, w_b_out, w_o, w_up, w_down, w_pe, w_pg) = lw
    B, T, _ = x.shape
    h = rmsnorm(x, norm_mix)
    z = h @ w_in
    xa, ga, q, k, v, fl, gate_a, gate_b = split_in(z)
    xc, conv_new = causal_conv(xa, conv_prev, conv_w, conv_b)
    hs, h_last = rg_lru(xc, h0, w_rg, b_rg, w_ig, b_ig, lru_lambda)
    y_a = (hs * jax.nn.gelu(ga.astype(F32))).astype(x.dtype)
    def heads(t):
        return t.reshape(B, T, N_HEADS, HEAD_DIM).transpose(0, 2, 1, 3)
    q = rmsnorm(heads(q), q_gain)
    k = rmsnorm(heads(k), k_gain)
    v = heads(v)
    logf = jax.nn.log_sigmoid(fl.astype(F32) + b_f.astype(F32)).transpose(0, 2, 1)
    if kv_past is None:
        o = fox_prompt(q, k, v, logf)
    else:
        o = fox_sample(q, k, v, logf, *kv_past)
    y_b = o.transpose(0, 2, 1, 3).reshape(B, T, D_ATTN)
    merged = jax.nn.sigmoid(gate_a) * (y_a @ w_a_out) + jax.nn.sigmoid(gate_b) * (y_b @ w_b_out)
    x = x + merged @ w_o
    h2 = rmsnorm(x, norm_mlp)
    x = x + jnp.square(jax.nn.relu(h2 @ w_up)) @ w_down
    g = jax.nn.sigmoid(rmsnorm(x, norm_ple) @ w_pg)
    x = x + g * (p_i @ w_pe)
    return x, (k, v, logf, conv_new, h_last)


def setup_inputs(seed: int = 0) -> dict:
    key = jax.random.key(seed)
    ks = jax.random.split(key, 40)

    def nrm(k, shape, scale):
        return scale * jax.random.normal(k, shape, F32)

    def uni(k, shape, lo, hi):
        return jax.random.uniform(k, shape, F32, lo, hi)

    a8 = uni(ks[20], (DEPTH, D_RNN), 0.9, 0.999)
    a = a8 ** (1.0 / LRU_C)
    lru_lambda = jnp.log(a) - jnp.log1p(-a)
    cache_logf = jax.nn.log_sigmoid(uni(ks[4], (DEPTH, DEC_BATCH, N_HEADS, PAST_LEN), 3.0, 6.0)
                                    + nrm(ks[5], (DEPTH, DEC_BATCH, N_HEADS, PAST_LEN), 1.0))
    return {
        'x_prompt': nrm(ks[0], (BATCH, SEQ, D_MODEL), 1.0),
        'x_sample': nrm(ks[1], (DEC_BATCH, DEC_SEQ, D_MODEL), 1.0),
        'cache_k': nrm(ks[2], (DEPTH, DEC_BATCH, N_HEADS, PAST_LEN, HEAD_DIM), 1.0),
        'cache_v': nrm(ks[3], (DEPTH, DEC_BATCH, N_HEADS, PAST_LEN, HEAD_DIM), 1.0),
        'cache_logf': cache_logf,
        'state_conv': nrm(ks[6], (DEPTH, DEC_BATCH, CONV_WIDTH - 1, D_RNN), 1.0),
        'state_lru': nrm(ks[7], (DEPTH, DEC_BATCH, D_RNN), 0.5),
        'p_prompt': nrm(ks[8], (DEPTH, BATCH, SEQ, D_PLE), 1.0),
        'p_sample': nrm(ks[9], (DEPTH, DEC_BATCH, DEC_SEQ, D_PLE), 1.0),
        'norm_mix': 1.0 + nrm(ks[10], (DEPTH, D_MODEL), 0.05),
        'norm_mlp': 1.0 + nrm(ks[11], (DEPTH, D_MODEL), 0.05),
        'norm_ple': 1.0 + nrm(ks[12], (DEPTH, D_MODEL), 0.05),
        'w_in': nrm(ks[13], (DEPTH, D_MODEL, N_IN), D_MODEL ** -0.5),
        'b_f': uni(ks[14], (DEPTH, N_HEADS), 3.0, 6.0),
        'conv_w': nrm(ks[15], (DEPTH, CONV_WIDTH, D_RNN), CONV_WIDTH ** -0.5),
        'conv_b': nrm(ks[16], (DEPTH, D_RNN), 0.01),
        'w_rg': nrm(ks[17], (DEPTH, N_LRU_BLOCKS, LRU_BLOCK, LRU_BLOCK), LRU_BLOCK ** -0.5),
        'b_rg': nrm(ks[18], (DEPTH, D_RNN), 0.01),
        'w_ig': nrm(ks[19], (DEPTH, N_LRU_BLOCKS, LRU_BLOCK, LRU_BLOCK), LRU_BLOCK ** -0.5),
        'b_ig': nrm(ks[21], (DEPTH, D_RNN), 0.01),
        'lru_lambda': lru_lambda,
        'q_gain': 1.0 + nrm(ks[22], (DEPTH, HEAD_DIM), 0.05),
        'k_gain': 1.0 + nrm(ks[23], (DEPTH, HEAD_DIM), 0.05),
        'w_a_out': nrm(ks[24], (DEPTH, D_RNN, D_MODEL), D_RNN ** -0.5),
        'w_b_out': nrm(ks[25], (DEPTH, D_ATTN, D_MODEL), D_ATTN ** -0.5),
        'w_o': nrm(ks[26], (DEPTH, D_MODEL, D_MODEL), D_MODEL ** -0.5),
        'w_up': nrm(ks[27], (DEPTH, D_MODEL, D_FF), D_MODEL ** -0.5),
        'w_down': nrm(ks[28], (DEPTH, D_FF, D_MODEL), D_FF ** -0.5),
        'w_pe': nrm(ks[29], (DEPTH, D_PLE, D_MODEL), D_PLE ** -0.5),
        'w_pg': nrm(ks[30], (DEPTH, D_MODEL, D_MODEL), D_MODEL ** -0.5),
    }


def reference(x_prompt, x_sample, cache_k, cache_v, cache_logf, state_conv, state_lru,
              p_prompt, p_sample, norm_mix, norm_mlp, norm_ple, w_in, b_f, conv_w, conv_b,
              w_rg, b_rg, w_ig, b_ig, lru_lambda, q_gain, k_gain, w_a_out, w_b_out, w_o,
              w_up, w_down, w_pe, w_pg):
    xp = x_prompt
    xs = x_sample
    kp, vp, fp, cp, lp = [], [], [], [], []
    kss, vss, fss, css, lss = [], [], [], [], []
    for i in range(DEPTH):
        lw = (norm_mix[i], norm_mlp[i], norm_ple[i], w_in[i], b_f[i], conv_w[i], conv_b[i],
              w_rg[i], b_rg[i], w_ig[i], b_ig[i], lru_lambda[i], q_gain[i], k_gain[i],
              w_a_out[i], w_b_out[i], w_o[i], w_up[i], w_down[i], w_pe[i], w_pg[i])
        conv0 = jnp.zeros((xp.shape[0], CONV_WIDTH - 1, D_RNN), xp.dtype)
        h0 = jnp.zeros((xp.shape[0], D_RNN), F32)
        xp, (k1, v1, f1, c1, l1) = trunk_layer(xp, p_prompt[i], conv0, h0, None, lw)
        kp.append(k1); vp.append(v1); fp.append(f1); cp.append(c1); lp.append(l1)
        xs, (k2, v2, f2, c2, l2) = trunk_layer(xs, p_sample[i], state_conv[i], state_lru[i],
                                               (cache_k[i], cache_v[i], cache_logf[i]), lw)
        kss.append(k2); vss.append(v2); fss.append(f2); css.append(c2); lss.append(l2)
    return (xp, xs,
            jnp.stack(kp), jnp.stack(vp), jnp.stack(fp), jnp.stack(cp), jnp.stack(lp),
            jnp.stack(kss), jnp.stack(vss), jnp.stack(fss), jnp.stack(css), jnp.stack(lss))
```

```python
import functools

import jax
import jax.numpy as jnp
from jax import lax
from jax.experimental import pallas as pl
from jax.experimental.pallas import tpu as pltpu

F32 = jnp.float32
BF16 = jnp.bfloat16

EPS = 1e-6
LRU_C = 8.0
CONV_WIDTH = 4
LANES = 128
SUBLANES = 8
AUG = 128
NEG_BIG = -1e30
VMEM_LIMIT = 52 * 1024 * 1024


def _params(*sem):
    return pltpu.CompilerParams(dimension_semantics=sem, vmem_limit_bytes=VMEM_LIMIT)


def _rms(x, g):
    return x * lax.rsqrt(jnp.mean(x * x, axis=-1, keepdims=True) + EPS) * g


def _log_sigmoid(x):
    return jnp.minimum(x, 0.0) - jnp.log1p(jnp.exp(-jnp.abs(x)))


def _sigmoid(x):
    return 1.0 / (1.0 + jnp.exp(-x))


def _split3(c):
    hi = c.astype(BF16).astype(F32)
    r = c - hi
    mid = r.astype(BF16).astype(F32)
    lo = (r - mid).astype(BF16).astype(F32)
    return hi, mid, lo


def _aug_q(c, lane):
    hi, mid, lo = _split3(c)
    return jnp.where(lane == 0, hi, jnp.where(lane == 1, mid, jnp.where(
        lane == 2, lo, jnp.where(lane < 6, 1.0, 0.0))))


def _aug_k(c, lane):
    hi, mid, lo = _split3(c)
    return jnp.where(lane < 3, 1.0, jnp.where(lane == 3, -hi, jnp.where(
        lane == 4, -mid, jnp.where(lane == 5, -lo, 0.0))))


def _cumsum_rows(x):
    n = x.shape[0]
    row = lax.broadcasted_iota(jnp.int32, x.shape, 0)
    d = 1
    while d < n:
        x = x + jnp.where(row >= d, pltpu.roll(x, d, 0), 0.0)
        d *= 2
    return x


def _normed_matmul_kernel(x_ref, g_ref, w_ref, o_ref, xn_ref):
    @pl.when(pl.program_id(1) == 0)
    def _():
        xn_ref[...] = _rms(x_ref[...], g_ref[...]).astype(BF16)

    o_ref[...] = jnp.dot(xn_ref[...], w_ref[...], preferred_element_type=F32)


def _normed_matmul(x, g, w, layer, *, tm, tn):
    n, d = x.shape
    n_out = w.shape[-1]
    return pl.pallas_call(
        _normed_matmul_kernel,
        out_shape=jax.ShapeDtypeStruct((n, n_out), F32),
        grid=(n // tm, n_out // tn),
        in_specs=[
            pl.BlockSpec((tm, d), lambda i, j: (i, 0)),
            pl.BlockSpec((None, 1, d), lambda i, j: (layer, 0, 0)),
            pl.BlockSpec((None, d, tn), lambda i, j: (layer, 0, j)),
        ],
        out_specs=pl.BlockSpec((tm, tn), lambda i, j: (i, j)),
        scratch_shapes=[pltpu.VMEM((tm, d), BF16)],
        compiler_params=_params("parallel", "arbitrary"),
        name="normed_matmul",
    )(x, g, w)


def _qkv_prompt_kernel(x_ref, g_ref, w_ref, wfl_ref, bf_ref, qg_ref, kg_ref,
                       qp_ref, kp_ref, vb_ref, ko_ref, vo_ref, lf_ref,
                       xn_ref, c_ref, carry_ref, *, n_heads, scale):
    i = pl.program_id(1)
    j = pl.program_id(2)
    tm = xn_ref.shape[0]

    @pl.when(j == 0)
    def _():
        xn = _rms(x_ref[...], g_ref[...]).astype(BF16)
        xn_ref[...] = xn
        fl = jnp.dot(xn, wfl_ref[...], preferred_element_type=F32)
        logf = _log_sigmoid(fl + bf_ref[...])
        lf_ref[...] = logf

        @pl.when(i == 0)
        def _():
            carry_ref[...] = jnp.zeros_like(carry_ref)

        c = _cumsum_rows(logf) + carry_ref[...]
        carry_ref[...] = c[tm - 1:tm, :]
        c_ref[...] = c

    z = jnp.dot(xn_ref[...], w_ref[...], preferred_element_type=F32)
    lane = lax.broadcasted_iota(jnp.int32, (tm, LANES), 1)

    def head_c(h):
        return jnp.broadcast_to(c_ref[:, h:h + 1], (tm, LANES))

    @pl.when(j == 0)
    def _():
        for h in range(n_heads):
            zh = z[:, h * LANES:(h + 1) * LANES]
            qp_ref[h, :, 0:LANES] = (_rms(zh, qg_ref[...]) * scale).astype(BF16)
            qp_ref[h, :, LANES:LANES + AUG] = _aug_q(head_c(h), lane).astype(BF16)

    @pl.when(j == 1)
    def _():
        for h in range(n_heads):
            zh = z[:, h * LANES:(h + 1) * LANES]
            kn = _rms(zh, kg_ref[...])
            ko_ref[h] = kn
            kp_ref[h, :, 0:LANES] = kn.astype(BF16)
            kp_ref[h, :, LANES:LANES + AUG] = _aug_k(head_c(h), lane).astype(BF16)

    @pl.when(j == 2)
    def _():
        for h in range(n_heads):
            zh = z[:, h * LANES:(h + 1) * LANES]
            vo_ref[h] = zh
            vb_ref[h] = zh.astype(BF16)


def _qkv_prompt(x, g, w_qkv, w_fl, b_f, q_gain, k_gain, layer, *, n_heads, tm):
    b, s, d = x.shape
    hd = LANES
    d_attn = n_heads * hd
    scale = float(hd) ** -0.5
    head_major = lambda width, dt: jax.ShapeDtypeStruct((b, n_heads, s, width), dt)
    head_spec = lambda width: pl.BlockSpec((None, n_heads, tm, width), lambda bi, i, j: (bi, 0, i, 0))
    return pl.pallas_call(
        functools.partial(_qkv_prompt_kernel, n_heads=n_heads, scale=scale),
        out_shape=(head_major(hd + AUG, BF16), head_major(hd + AUG, BF16), head_major(hd, BF16),
                   head_major(hd, F32), head_major(hd, F32),
                   jax.ShapeDtypeStruct((b, s, LANES), F32)),
        grid=(b, s // tm, 3),
        in_specs=[
            pl.BlockSpec((None, tm, d), lambda bi, i, j: (bi, i, 0)),
            pl.BlockSpec((None, 1, d), lambda bi, i, j: (layer, 0, 0)),
            pl.BlockSpec((None, d, d_attn), lambda bi, i, j: (layer, 0, j)),
            pl.BlockSpec((None, d, LANES), lambda bi, i, j: (layer, 0, 0)),
            pl.BlockSpec((None, 1, LANES), lambda bi, i, j: (layer, 0, 0)),
            pl.BlockSpec((None, 1, hd), lambda bi, i, j: (layer, 0, 0)),
            pl.BlockSpec((None, 1, hd), lambda bi, i, j: (layer, 0, 0)),
        ],
        out_specs=(head_spec(hd + AUG), head_spec(hd + AUG), head_spec(hd),
                   head_spec(hd), head_spec(hd),
                   pl.BlockSpec((None, tm, LANES), lambda bi, i, j: (bi, i, 0))),
        scratch_shapes=[pltpu.VMEM((tm, d), BF16), pltpu.VMEM((tm, LANES), F32),
                        pltpu.VMEM((1, LANES), F32)],
        compiler_params=_params("parallel", "arbitrary", "arbitrary"),
        name="qkv_prompt",
    )(x, g, w_qkv, w_fl, b_f, q_gain, k_gain)


def _rglru_kernel(xa_ref, ga_ref, cprev_ref, h0_ref, cw_ref, cb_ref, wr_ref, br_ref,
                  wi_ref, bi_ref, lam_ref, ya_ref, cnew_ref, hlast_ref,
                  xpad_ref, a_ref, u_ref, hs_ref, hc_ref, *, n_blocks):
    t = pl.program_id(1)
    tc = xa_ref.shape[0]
    pad = SUBLANES

    @pl.when(t == 0)
    def _():
        xpad_ref[pad - (CONV_WIDTH - 1):pad, :] = cprev_ref[...]
        hc_ref[...] = h0_ref[...]

    xa = xa_ref[...]
    xpad_ref[pad:pad + tc, :] = xa
    xc = cb_ref[...] + cw_ref[CONV_WIDTH - 1:CONV_WIDTH, :] * xa
    for jj in range(CONV_WIDTH - 1):
        off = pad - (CONV_WIDTH - 1) + jj
        xc = xc + cw_ref[jj:jj + 1, :] * xpad_ref[off:off + tc, :]
    tail = xa[tc - (CONV_WIDTH - 1):tc, :]
    xpad_ref[pad - (CONV_WIDTH - 1):pad, :] = tail
    cnew_ref[...] = tail

    lam = lam_ref[...]
    softplus_neg_lam = jnp.maximum(-lam, 0.0) + jnp.log1p(jnp.exp(-jnp.abs(lam)))
    for nb in range(n_blocks):
        sl = slice(nb * LANES, (nb + 1) * LANES)
        xb = xc[:, sl]
        xb16 = xb.astype(BF16)
        r = _sigmoid(jnp.dot(xb16, wr_ref[nb], preferred_element_type=F32) + br_ref[:, sl])
        ig = _sigmoid(jnp.dot(xb16, wi_ref[nb], preferred_element_type=F32) + bi_ref[:, sl])
        log_a = (-LRU_C) * r * softplus_neg_lam[:, sl]
        a = jnp.exp(log_a)
        a_ref[:, sl] = a
        u_ref[:, sl] = jnp.sqrt(jnp.tanh(-log_a) * (1.0 + a * a)) * (ig * xb)

    def step(g, h):
        base = pl.multiple_of(g * SUBLANES, SUBLANES)
        for r_ in range(SUBLANES):
            h = a_ref[pl.ds(base + r_, 1), :] * h + u_ref[pl.ds(base + r_, 1), :]
            hs_ref[pl.ds(base + r_, 1), :] = h
        return h

    h_fin = lax.fori_loop(0, tc // SUBLANES, step, hc_ref[...])
    hc_ref[...] = h_fin
    hlast_ref[...] = h_fin
    ya_ref[...] = (hs_ref[...] * jax.nn.gelu(ga_ref[...])).astype(BF16)


def _rglru(z, conv_prev, h0, conv_w, conv_b, w_r, b_r, w_i, b_i, lam, layer, *, tc):
    b, t, _ = z.shape
    d_rnn = conv_w.shape[-1]
    n_blocks = w_r.shape[1]
    vec = lambda: pl.BlockSpec((None, 1, d_rnn), lambda bi, ti: (layer, 0, 0))
    gate_w = lambda: pl.BlockSpec((None, n_blocks, LANES, LANES), lambda bi, ti: (layer, 0, 0, 0))
    return pl.pallas_call(
        functools.partial(_rglru_kernel, n_blocks=n_blocks),
        out_shape=(jax.ShapeDtypeStruct((b, t, d_rnn), BF16),
                   jax.ShapeDtypeStruct((b, CONV_WIDTH - 1, d_rnn), F32),
                   jax.ShapeDtypeStruct((b, 1, d_rnn), F32)),
        grid=(b, t // tc),
        in_specs=[
            pl.BlockSpec((None, tc, d_rnn), lambda bi, ti: (bi, ti, 0)),
            pl.BlockSpec((None, tc, d_rnn), lambda bi, ti: (bi, ti, 1)),
            pl.BlockSpec((None, CONV_WIDTH - 1, d_rnn), lambda bi, ti: (bi, 0, 0)),
            pl.BlockSpec((None, 1, d_rnn), lambda bi, ti: (bi, 0, 0)),
            pl.BlockSpec((None, CONV_WIDTH, d_rnn), lambda bi, ti: (layer, 0, 0)),
            vec(), gate_w(), vec(), gate_w(), vec(), vec(),
        ],
        out_specs=(pl.BlockSpec((None, tc, d_rnn), lambda bi, ti: (bi, ti, 0)),
                   pl.BlockSpec((None, CONV_WIDTH - 1, d_rnn), lambda bi, ti: (bi, 0, 0)),
                   pl.BlockSpec((None, 1, d_rnn), lambda bi, ti: (bi, 0, 0))),
        scratch_shapes=[pltpu.VMEM((tc + SUBLANES, d_rnn), F32), pltpu.VMEM((tc, d_rnn), F32),
                        pltpu.VMEM((tc, d_rnn), F32), pltpu.VMEM((tc, d_rnn), F32),
                        pltpu.VMEM((1, d_rnn), F32)],
        compiler_params=_params("parallel", "arbitrary"),
        name="rglru",
    )(z, z, conv_prev, h0, conv_w, conv_b, w_r, b_r, w_i, b_i, lam)


def _attn_prompt_kernel(q_ref, k_ref, v_ref, o_ref, m_ref, l_ref, acc_ref):
    qi = pl.program_id(2)
    tq = q_ref.shape[0]
    q = q_ref[...]
    reps = tq // LANES

    def scores(start):
        k = k_ref[pl.ds(start, tq), :]
        return lax.dot_general(q, k, (((1,), (1,)), ((), ())), preferred_element_type=F32)

    d0 = pl.multiple_of(qi * tq, tq)
    s = scores(d0)
    row = lax.broadcasted_iota(jnp.int32, (tq, tq), 0)
    col = lax.broadcasted_iota(jnp.int32, (tq, tq), 1)
    s = jnp.where(col <= row, s, NEG_BIG)
    m0 = jnp.max(s, axis=1, keepdims=True)
    p = jnp.exp(s - m0)
    m_ref[...] = jnp.broadcast_to(m0, m_ref.shape)
    l_ref[...] = jnp.broadcast_to(jnp.sum(p, axis=1, keepdims=True), l_ref.shape)
    acc_ref[...] = jnp.dot(p.astype(BF16), v_ref[pl.ds(d0, tq), :], preferred_element_type=F32)

    def body(jb, carry):
        start = pl.multiple_of(jb * tq, tq)
        s = scores(start)
        m_prev = m_ref[...]
        m_new = jnp.maximum(m_prev, jnp.max(s, axis=1, keepdims=True))
        alpha = jnp.exp(m_prev - m_new)
        p = jnp.exp(s - pltpu.repeat(m_new, reps, 1))
        l_ref[...] = alpha * l_ref[...] + jnp.sum(p, axis=1, keepdims=True)
        acc_ref[...] = alpha * acc_ref[...] + jnp.dot(
            p.astype(BF16), v_ref[pl.ds(start, tq), :], preferred_element_type=F32)
        m_ref[...] = m_new
        return carry

    lax.fori_loop(0, qi, body, 0)
    o_ref[...] = (acc_ref[...] / l_ref[...]).astype(o_ref.dtype)


def _attn_prompt(qp, kp, vb, *, tq):
    b, h, s, dq = qp.shape
    hd = vb.shape[-1]
    return pl.pallas_call(
        _attn_prompt_kernel,
        out_shape=jax.ShapeDtypeStruct((b, s, h * hd), BF16),
        grid=(b, h, s // tq),
        in_specs=[
            pl.BlockSpec((None, None, tq, dq), lambda bi, hi, qi: (bi, hi, qi, 0)),
            pl.BlockSpec((None, None, s, dq), lambda bi, hi, qi: (bi, hi, 0, 0)),
            pl.BlockSpec((None, None, s, hd), lambda bi, hi, qi: (bi, hi, 0, 0)),
        ],
        out_specs=pl.BlockSpec((None, tq, hd), lambda bi, hi, qi: (bi, qi, hi)),
        scratch_shapes=[pltpu.VMEM((tq, LANES), F32), pltpu.VMEM((tq, LANES), F32),
                        pltpu.VMEM((tq, hd), F32)],
        compiler_params=_params("parallel", "parallel", "arbitrary"),
        name="attn_prompt",
    )(qp, kp, vb)


def _cumsum_lanes_kernel(x_ref, o_ref):
    rows, n = x_ref.shape
    r_i = lax.broadcasted_iota(jnp.int32, (LANES, LANES), 0)
    c_i = lax.broadcasted_iota(jnp.int32, (LANES, LANES), 1)
    upper = jnp.where(r_i <= c_i, 1.0, 0.0).astype(BF16)
    carry = jnp.zeros((rows, 1), F32)
    for ch in range(n // LANES):
        sl = slice(ch * LANES, (ch + 1) * LANES)
        cs = carry
        for piece in _split3(x_ref[:, sl]):
            cs = cs + jnp.dot(piece.astype(BF16), upper, preferred_element_type=F32)
        o_ref[:, sl] = cs
        carry = cs[:, LANES - 1:LANES]


def _cumsum_lanes(x, *, tr):
    rows, n = x.shape
    return pl.pallas_call(
        _cumsum_lanes_kernel,
        out_shape=jax.ShapeDtypeStruct((rows, n), F32),
        grid=(rows // tr,),
        in_specs=[pl.BlockSpec((tr, n), lambda i: (i, 0))],
        out_specs=pl.BlockSpec((tr, n), lambda i: (i, 0)),
        compiler_params=_params("parallel"),
        name="cumsum_lanes",
    )(x)


def _attn_sample_kernel(zq_ref, zk_ref, zv_ref, fl_ref, bf_ref, qg_ref, kg_ref,
                        kc_ref, vc_ref, cp_ref, yb_ref, ko_ref, vo_ref, lf_ref, *, scale):
    h = pl.program_id(1)
    t = zq_ref.shape[0]
    p_len = kc_ref.shape[0]
    lane = lax.broadcasted_iota(jnp.int32, (t, LANES), 1)

    logf_all = _log_sigmoid(fl_ref[...] + bf_ref[...])
    lf_ref[...] = logf_all
    logf = jnp.sum(jnp.where(lane == h, logf_all, 0.0), axis=1, keepdims=True)
    cl = _cumsum_rows(jnp.broadcast_to(logf, (t, LANES)))

    qn = _rms(zq_ref[...], qg_ref[...]) * scale
    kn = _rms(zk_ref[...], kg_ref[...])
    v = zv_ref[...]
    ko_ref[...] = kn
    vo_ref[...] = v

    cp = cp_ref[...]
    kc = kc_ref[...].astype(BF16)
    s_p = lax.dot_general(qn.astype(BF16), kc, (((1,), (1,)), ((), ())), preferred_element_type=F32)
    s_p = s_p + cl[:, 0:1] + (cp[:, p_len - 1:p_len] - cp)

    q_aug = jnp.concatenate([qn, _aug_q(cl, lane)], axis=1).astype(BF16)
    k_aug = jnp.concatenate([kn, _aug_k(cl, lane)], axis=1).astype(BF16)
    k_pad = jnp.concatenate([k_aug, jnp.zeros((LANES - t, 2 * LANES), BF16)], axis=0)
    v_pad = jnp.concatenate([v.astype(BF16), jnp.zeros((LANES - t, LANES), BF16)], axis=0)
    s_n = lax.dot_general(q_aug, k_pad, (((1,), (1,)), ((), ())), preferred_element_type=F32)
    row = lax.broadcasted_iota(jnp.int32, (t, LANES), 0)
    s_n = jnp.where(lane <= row, s_n, NEG_BIG)

    m = jnp.maximum(jnp.max(s_p, axis=1, keepdims=True), jnp.max(s_n, axis=1, keepdims=True))
    p_p = jnp.exp(s_p - m)
    p_n = jnp.exp(s_n - m)
    l = jnp.sum(p_p, axis=1, keepdims=True) + jnp.sum(p_n, axis=1, keepdims=True)
    o = jnp.dot(p_p.astype(BF16), vc_ref[...].astype(BF16), preferred_element_type=F32)
    o = o + jnp.dot(p_n.astype(BF16), v_pad, preferred_element_type=F32)
    yb_ref[...] = (o / l).astype(yb_ref.dtype)


def _attn_sample(z, b_f, q_gain, k_gain, cache_k, cache_v, c_past, layer, *, n_heads, col0):
    b, t, _ = z.shape
    hd = LANES
    p_len = cache_k.shape[-2]
    scale = float(hd) ** -0.5
    zspec = lambda sec: pl.BlockSpec((None, t, hd), lambda bi, hi: (bi, 0, col0 + sec * n_heads + hi))
    cache = lambda: pl.BlockSpec((None, None, None, p_len, hd), lambda bi, hi: (layer, bi, hi, 0, 0))
    return pl.pallas_call(
        functools.partial(_attn_sample_kernel, scale=scale),
        out_shape=(jax.ShapeDtypeStruct((b, t, n_heads * hd), BF16),
                   jax.ShapeDtypeStruct((b, n_heads, t, hd), F32),
                   jax.ShapeDtypeStruct((b, n_heads, t, hd), F32),
                   jax.ShapeDtypeStruct((b, t, LANES), F32)),
        grid=(b, n_heads),
        in_specs=[
            zspec(0), zspec(1), zspec(2),
            pl.BlockSpec((None, t, LANES), lambda bi, hi: (bi, 0, col0 + 3 * n_heads)),
            pl.BlockSpec((None, 1, LANES), lambda bi, hi: (layer, 0, 0)),
            pl.BlockSpec((None, 1, hd), lambda bi, hi: (layer, 0, 0)),
            pl.BlockSpec((None, 1, hd), lambda bi, hi: (layer, 0, 0)),
            cache(), cache(),
            pl.BlockSpec((None, None, None, 1, p_len), lambda bi, hi: (layer, bi, hi, 0, 0)),
        ],
        out_specs=(pl.BlockSpec((None, t, hd), lambda bi, hi: (bi, 0, hi)),
                   pl.BlockSpec((None, None, t, hd), lambda bi, hi: (bi, hi, 0, 0)),
                   pl.BlockSpec((None, None, t, hd), lambda bi, hi: (bi, hi, 0, 0)),
                   pl.BlockSpec((None, t, LANES), lambda bi, hi: (bi, 0, 0))),
        compiler_params=_params("parallel", "arbitrary"),
        name="attn_sample",
    )(z, z, z, z, b_f, q_gain, k_gain, cache_k, cache_v, c_past)


def _merge_kernel(ya_ref, yb_ref, ga_ref, gb_ref, wa_ref, wb_ref, o_ref):
    a = jnp.dot(ya_ref[...], wa_ref[...], preferred_element_type=F32)
    b = jnp.dot(yb_ref[...], wb_ref[...], preferred_element_type=F32)
    o_ref[...] = (_sigmoid(ga_ref[...]) * a + _sigmoid(gb_ref[...]) * b).astype(o_ref.dtype)


def _merge(ya, yb, z, w_a, w_b, layer, *, gate_col, tm, tn):
    n, d_a = ya.shape
    d_b = yb.shape[-1]
    d = w_a.shape[-1]
    ga0 = gate_col // tn
    gb0 = (gate_col + d) // tn
    return pl.pallas_call(
        _merge_kernel,
        out_shape=jax.ShapeDtypeStruct((n, d), BF16),
        grid=(n // tm, d // tn),
        in_specs=[
            pl.BlockSpec((tm, d_a), lambda i, j: (i, 0)),
            pl.BlockSpec((tm, d_b), lambda i, j: (i, 0)),
            pl.BlockSpec((tm, tn), lambda i, j: (i, ga0 + j)),
            pl.BlockSpec((tm, tn), lambda i, j: (i, gb0 + j)),
            pl.BlockSpec((None, d_a, tn), lambda i, j: (layer, 0, j)),
            pl.BlockSpec((None, d_b, tn), lambda i, j: (layer, 0, j)),
        ],
        out_specs=pl.BlockSpec((tm, tn), lambda i, j: (i, j)),
        compiler_params=_params("parallel", "arbitrary"),
        name="merge",
    )(ya, yb, z, z, w_a, w_b)


def _matmul_res_kernel(m_ref, w_ref, x_ref, o_ref):
    o_ref[...] = x_ref[...] + jnp.dot(m_ref[...], w_ref[...], preferred_element_type=F32)


def _matmul_res(m, w, x, layer, *, tm, tn):
    n, k = m.shape
    d = w.shape[-1]
    return pl.pallas_call(
        _matmul_res_kernel,
        out_shape=jax.ShapeDtypeStruct((n, d), F32),
        grid=(n // tm, d // tn),
        in_specs=[
            pl.BlockSpec((tm, k), lambda i, j: (i, 0)),
            pl.BlockSpec((None, k, tn), lambda i, j: (layer, 0, j)),
            pl.BlockSpec((tm, tn), lambda i, j: (i, j)),
        ],
        out_specs=pl.BlockSpec((tm, tn), lambda i, j: (i, j)),
        compiler_params=_params("parallel", "arbitrary"),
        name="matmul_res",
    )(m, w, x)


def _mlp_kernel(x_ref, g_ref, wu_ref, wd_ref, o_ref, xn_ref, acc_ref):
    f = pl.program_id(1)

    @pl.when(f == 0)
    def _():
        x = x_ref[...]
        xn_ref[...] = _rms(x, g_ref[...]).astype(BF16)
        acc_ref[...] = x

    hid = jnp.maximum(jnp.dot(xn_ref[...], wu_ref[...], preferred_element_type=F32), 0.0)
    acc_ref[...] += jnp.dot((hid * hid).astype(BF16), wd_ref[...], preferred_element_type=F32)

    @pl.when(f == pl.num_programs(1) - 1)
    def _():
        o_ref[...] = acc_ref[...]


def _mlp(x, g, w_up, w_down, layer, *, tm, tf):
    n, d = x.shape
    d_ff = w_up.shape[-1]
    return pl.pallas_call(
        _mlp_kernel,
        out_shape=jax.ShapeDtypeStruct((n, d), F32),
        grid=(n // tm, d_ff // tf),
        in_specs=[
            pl.BlockSpec((tm, d), lambda i, f: (i, 0)),
            pl.BlockSpec((None, 1, d), lambda i, f: (layer, 0, 0)),
            pl.BlockSpec((None, d, tf), lambda i, f: (layer, 0, f)),
            pl.BlockSpec((None, tf, d), lambda i, f: (layer, f, 0)),
        ],
        out_specs=pl.BlockSpec((tm, d), lambda i, f: (i, 0)),
        scratch_shapes=[pltpu.VMEM((tm, d), BF16), pltpu.VMEM((tm, d), F32)],
        compiler_params=_params("parallel", "arbitrary"),
        name="mlp",
    )(x, g, w_up, w_down)


def _ple_kernel(x_ref, xcol_ref, g_ref, wg_ref, p_ref, we_ref, o_ref, xn_ref):
    @pl.when(pl.program_id(1) == 0)
    def _():
        xn_ref[...] = _rms(x_ref[...], g_ref[...]).astype(BF16)

    gate = _sigmoid(jnp.dot(xn_ref[...], wg_ref[...], preferred_element_type=F32))
    emb = jnp.dot(p_ref[...].astype(BF16), we_ref[...], preferred_element_type=F32)
    o_ref[...] = xcol_ref[...] + gate * emb


def _ple(x, g, w_pg, p, w_pe, layer, *, tm, tn):
    n, d = x.shape
    d_ple = p.shape[-1]
    return pl.pallas_call(
        _ple_kernel,
        out_shape=jax.ShapeDtypeStruct((n, d), F32),
        grid=(n // tm, d // tn),
        in_specs=[
            pl.BlockSpec((tm, d), lambda i, j: (i, 0)),
            pl.BlockSpec((tm, tn), lambda i, j: (i, j)),
            pl.BlockSpec((None, 1, d), lambda i, j: (layer, 0, 0)),
            pl.BlockSpec((None, d, tn), lambda i, j: (layer, 0, j)),
            pl.BlockSpec((None, tm, d_ple), lambda i, j: (layer, i, 0)),
            pl.BlockSpec((None, d_ple, tn), lambda i, j: (layer, 0, j)),
        ],
        out_specs=pl.BlockSpec((tm, tn), lambda i, j: (i, j)),
        scratch_shapes=[pltpu.VMEM((tm, d), BF16)],
        compiler_params=_params("parallel", "arbitrary"),
        name="ple",
    )(x, x, g, w_pg, p, w_pe)


def _tile(n, pref):
    t = min(n, pref)
    while n % t:
        t //= 2
    return t


def kernel(x_prompt, x_sample, cache_k, cache_v, cache_logf, state_conv, state_lru, p_prompt, p_sample, norm_mix, norm_mlp, norm_ple, w_in, b_f, conv_w, conv_b, w_rg, b_rg, w_ig, b_ig, lru_lambda, q_gain, k_gain, w_a_out, w_b_out, w_o, w_up, w_down, w_pe, w_pg):
    depth, d, _ = w_in.shape
    bp, sp, _ = x_prompt.shape
    bs, ts, _ = x_sample.shape
    d_rnn = conv_w.shape[-1]
    n_heads = b_f.shape[-1]
    hd = q_gain.shape[-1]
    d_attn = n_heads * hd
    d_ple = w_pe.shape[1]
    p_len = cache_k.shape[-2]
    assert hd == LANES and w_rg.shape[-1] == LANES and n_heads <= LANES

    o_q = 2 * d_rnn
    o_fl = o_q + 3 * d_attn
    o_gate = o_fl + n_heads
    w_rnn = w_in[:, :, :o_q]
    w_qkv = w_in[:, :, o_q:o_fl].astype(BF16)
    w_gates = w_in[:, :, o_gate:]
    w_fl = jnp.pad(w_in[:, :, o_fl:o_gate], ((0, 0), (0, 0), (0, LANES - n_heads)))
    w_ag = jnp.concatenate([w_rnn, w_gates], axis=-1).astype(BF16)
    fl_pad = _tile(d_attn, 1024)
    w_fl_wide = jnp.pad(w_fl, ((0, 0), (0, 0), (0, fl_pad - LANES)))
    w_all = jnp.concatenate([w_rnn, w_gates, w_in[:, :, o_q:o_fl], w_fl_wide], axis=-1).astype(BF16)
    w_fl = w_fl.astype(BF16)
    gate_col = o_q
    qkv_tile0 = (o_q + 2 * d) // LANES

    row = lambda a: a.reshape(depth, 1, a.shape[-1])
    bf_pad = jnp.pad(b_f, ((0, 0), (0, LANES - n_heads))).reshape(depth, 1, LANES)
    g_mix, g_mlp, g_ple = row(norm_mix), row(norm_mlp), row(norm_ple)
    qg, kg = row(q_gain), row(k_gain)
    cb, br, bi, lam = row(conv_b), row(b_rg), row(b_ig), row(lru_lambda)
    w_r16, w_i16 = w_rg.astype(BF16), w_ig.astype(BF16)
    w_a16, w_b16, w_o16 = w_a_out.astype(BF16), w_b_out.astype(BF16), w_o.astype(BF16)
    w_up16, w_dn16 = w_up.astype(BF16), w_down.astype(BF16)
    w_pe16, w_pg16 = w_pe.astype(BF16), w_pg.astype(BF16)

    n_p, n_s = bp * sp, bs * ts
    pp = p_prompt.reshape(depth, n_p, d_ple)
    ps = p_sample.reshape(depth, n_s, d_ple)

    rows = depth * bs * n_heads
    c_past = _cumsum_lanes(cache_logf.reshape(rows, p_len), tr=_tile(rows, 256))
    c_past = c_past.reshape(depth, bs, n_heads, 1, p_len)

    tn = _tile(d, 1024)
    tf = _tile(w_up.shape[-1], 1024)

    def tokenwise(x, y_a, y_b, z, pe, layer, tm):
        merged = _merge(y_a, y_b, z, w_a16, w_b16, layer, gate_col=gate_col, tm=tm, tn=tn)
        x = _matmul_res(merged, w_o16, x, layer, tm=tm, tn=tn)
        x = _mlp(x, g_mlp, w_up16, w_dn16, layer, tm=tm, tf=tf)
        return _ple(x, g_ple, w_pg16, pe, w_pe16, layer, tm=tm, tn=tn)

    xp = x_prompt.reshape(n_p, d)
    xs = x_sample.reshape(n_s, d)
    tm_p = _tile(sp, 512)
    tm_s = _tile(n_s, 512)
    conv0 = jnp.zeros((bp, CONV_WIDTH - 1, d_rnn), F32)
    h00 = jnp.zeros((bp, 1, d_rnn), F32)
    outs = [[] for _ in range(10)]
    for layer in range(depth):
        z = _normed_matmul(xp, g_mix, w_ag, layer, tm=tm_p, tn=tn)
        qp, kp, vb, k1, v1, lf1 = _qkv_prompt(xp.reshape(bp, sp, d), g_mix, w_qkv, w_fl, bf_pad, qg, kg,
                                              layer, n_heads=n_heads, tm=tm_p)
        y_a, c1, l1 = _rglru(z.reshape(bp, sp, -1), conv0, h00, conv_w, cb, w_r16, br, w_i16, bi, lam,
                             layer, tc=_tile(sp, 256))
        y_b = _attn_prompt(qp, kp, vb, tq=tm_p)
        xp = tokenwise(xp, y_a.reshape(n_p, d_rnn), y_b.reshape(n_p, d_attn), z, pp, layer, tm_p)
        f1 = lf1[:, :, :n_heads].transpose(0, 2, 1)

        zs = _normed_matmul(xs, g_mix, w_all, layer, tm=tm_s, tn=tn)
        zs3 = zs.reshape(bs, ts, -1)
        y_a, c2, l2 = _rglru(zs3, state_conv[layer], state_lru[layer].reshape(bs, 1, d_rnn), conv_w, cb,
                             w_r16, br, w_i16, bi, lam, layer, tc=ts)
        y_b, k2, v2, lf2 = _attn_sample(zs3, bf_pad, qg, kg, cache_k, cache_v, c_past, layer,
                                        n_heads=n_heads, col0=qkv_tile0)
        xs = tokenwise(xs, y_a.reshape(n_s, d_rnn), y_b.reshape(n_s, d_attn), zs, ps, layer, tm_s)
        f2 = lf2[:, :, :n_heads].transpose(0, 2, 1)

        for lst, val in zip(outs, (k1, v1, f1, c1, l1.reshape(bp, d_rnn),
                                   k2, v2, f2, c2, l2.reshape(bs, d_rnn))):
            lst.append(val)

    return (xp.reshape(bp, sp, d), xs.reshape(bs, ts, d)) + tuple(jnp.stack(o) for o in outs)
```

```python
import functools

import jax
import jax.numpy as jnp
from jax import lax
from jax.experimental import pallas as pl
from jax.experimental.pallas import tpu as pltpu

F32 = jnp.float32
BF16 = jnp.bfloat16

EPS = 1e-6
LRU_C = 8.0
CONV_WIDTH = 4
LANES = 128
SUBLANES = 8
AUG = 128
NEG_BIG = -1e30
LOG2_E = 1.4426950408889634
VMEM_LIMIT = 56 * 1024 * 1024


def _params(*sem):
    return pltpu.CompilerParams(dimension_semantics=sem, vmem_limit_bytes=VMEM_LIMIT)


def _rms(x, g):
    return x * lax.rsqrt(jnp.mean(x * x, axis=-1, keepdims=True) + EPS) * g


def _log_sigmoid(x):
    return jnp.minimum(x, 0.0) - jnp.log1p(jnp.exp(-jnp.abs(x)))


def _sigmoid(x):
    return 1.0 / (1.0 + jnp.exp(-x))


def _split3(c):
    hi = c.astype(BF16).astype(F32)
    r = c - hi
    mid = r.astype(BF16).astype(F32)
    lo = (r - mid).astype(BF16).astype(F32)
    return hi, mid, lo


def _aug_q(c, lane):
    hi, mid, lo = _split3(c)
    return jnp.where(lane == 0, hi, jnp.where(lane == 1, mid, jnp.where(
        lane == 2, lo, jnp.where(lane < 6, 1.0, 0.0))))


def _aug_k(c, lane):
    hi, mid, lo = _split3(c)
    return jnp.where(lane < 3, 1.0, jnp.where(lane == 3, -hi, jnp.where(
        lane == 4, -mid, jnp.where(lane == 5, -lo, 0.0))))


def _cumsum_rows(x):
    n = x.shape[0]
    row = lax.broadcasted_iota(jnp.int32, x.shape, 0)
    d = 1
    while d < n:
        x = x + jnp.where(row >= d, pltpu.roll(x, d, 0), 0.0)
        d *= 2
    return x


def _normed_matmul_kernel(x_ref, g_ref, w_ref, o_ref, xn_ref):
    @pl.when(pl.program_id(1) == 0)
    def _():
        xn_ref[...] = _rms(x_ref[...], g_ref[...]).astype(BF16)

    o_ref[...] = jnp.dot(xn_ref[...], w_ref[...], preferred_element_type=F32)


def _normed_matmul(x, g, w, layer, *, n_out, tm, tn):
    n, d = x.shape
    return pl.pallas_call(
        _normed_matmul_kernel,
        out_shape=jax.ShapeDtypeStruct((n, n_out), F32),
        grid=(n // tm, n_out // tn),
        in_specs=[
            pl.BlockSpec((tm, d), lambda i, j: (i, 0)),
            pl.BlockSpec((None, 1, d), lambda i, j: (layer, 0, 0)),
            pl.BlockSpec((None, d, tn), lambda i, j: (layer, 0, j)),
        ],
        out_specs=pl.BlockSpec((tm, tn), lambda i, j: (i, j)),
        scratch_shapes=[pltpu.VMEM((tm, d), BF16)],
        compiler_params=_params("parallel", "arbitrary"),
        name="normed_matmul",
    )(x, g, w)


def _qkv_prompt_kernel(x_ref, g_ref, wq_ref, wk_ref, wv_ref, wfl_ref, bf_ref, qg_ref, kg_ref,
                       qp_ref, kp_ref, vt_ref, ko_ref, vo_ref, lf_ref, carry_ref, *, n_heads, scale):
    tm = x_ref.shape[0]
    xn = _rms(x_ref[...], g_ref[...]).astype(BF16)
    logf = _log_sigmoid(jnp.dot(xn, wfl_ref[...], preferred_element_type=F32) + bf_ref[...])
    lf_ref[...] = logf

    @pl.when(pl.program_id(1) == 0)
    def _():
        carry_ref[...] = jnp.zeros_like(carry_ref)

    c = _cumsum_rows(logf) + carry_ref[...]
    carry_ref[...] = c[tm - 1:tm, :]
    hi, mid, lo = _split3(c * LOG2_E)
    lane = lax.broadcasted_iota(jnp.int32, (tm, LANES), 1)

    def to_lane(x, dst, h):
        return pltpu.roll(x, (dst - h) % LANES, 1)

    qg = qg_ref[...] * (scale * LOG2_E)
    zq = jnp.dot(xn, wq_ref[...], preferred_element_type=F32)
    for h in range(n_heads):
        aug = jnp.where(lane == 0, to_lane(hi, 0, h), jnp.where(lane == 1, to_lane(mid, 1, h), jnp.where(
            lane == 2, to_lane(lo, 2, h), jnp.where(lane < 6, 1.0, 0.0))))
        qp_ref[h, :, 0:LANES] = _rms(zq[:, h * LANES:(h + 1) * LANES], qg).astype(BF16)
        qp_ref[h, :, LANES:LANES + AUG] = aug.astype(BF16)

    zk = jnp.dot(xn, wk_ref[...], preferred_element_type=F32)
    for h in range(n_heads):
        aug = jnp.where(lane < 3, 1.0, jnp.where(lane == 3, -to_lane(hi, 3, h), jnp.where(
            lane == 4, -to_lane(mid, 4, h), jnp.where(lane == 5, -to_lane(lo, 5, h), 0.0))))
        kn = _rms(zk[:, h * LANES:(h + 1) * LANES], kg_ref[...])
        ko_ref[h] = kn
        kp_ref[h, :, 0:LANES] = kn.astype(BF16)
        kp_ref[h, :, LANES:LANES + AUG] = aug.astype(BF16)

    zv = jnp.dot(xn, wv_ref[...], preferred_element_type=F32)
    for h in range(n_heads):
        zh = zv[:, h * LANES:(h + 1) * LANES]
        vo_ref[h] = zh
        vt_ref[h] = zh.T.astype(BF16)


def _qkv_prompt(x, g, w, b_f, q_gain, k_gain, layer, *, n_heads, col0, tm):
    b, s, d = x.shape
    hd = LANES
    d_attn = n_heads * hd
    scale = float(hd) ** -0.5
    sec0 = col0 // n_heads
    head_major = lambda width, dt: jax.ShapeDtypeStruct((b, n_heads, s, width), dt)
    head_spec = lambda width: pl.BlockSpec((None, n_heads, tm, width), lambda bi, i: (bi, 0, i, 0))
    resident = lambda shape, idx: pl.BlockSpec(shape, idx, pipeline_mode=pl.Buffered(1))
    wsec = lambda sec: resident((None, d, d_attn), lambda bi, i: (layer, 0, sec0 + sec))
    return pl.pallas_call(
        functools.partial(_qkv_prompt_kernel, n_heads=n_heads, scale=scale),
        out_shape=(head_major(hd + AUG, BF16), head_major(hd + AUG, BF16),
                   jax.ShapeDtypeStruct((b, n_heads, s // tm, hd, tm), BF16),
                   head_major(hd, F32), head_major(hd, F32),
                   jax.ShapeDtypeStruct((b, s, LANES), F32)),
        grid=(b, s // tm),
        in_specs=[
            pl.BlockSpec((None, tm, d), lambda bi, i: (bi, i, 0)),
            pl.BlockSpec((None, 1, d), lambda bi, i: (layer, 0, 0)),
            wsec(0), wsec(1), wsec(2),
            resident((None, d, LANES), lambda bi, i: (layer, 0, col0 + 3 * n_heads)),
            pl.BlockSpec((None, 1, LANES), lambda bi, i: (layer, 0, 0)),
            pl.BlockSpec((None, 1, hd), lambda bi, i: (layer, 0, 0)),
            pl.BlockSpec((None, 1, hd), lambda bi, i: (layer, 0, 0)),
        ],
        out_specs=(head_spec(hd + AUG), head_spec(hd + AUG),
                   pl.BlockSpec((None, n_heads, None, hd, tm), lambda bi, i: (bi, 0, i, 0, 0)),
                   head_spec(hd), head_spec(hd),
                   pl.BlockSpec((None, tm, LANES), lambda bi, i: (bi, i, 0))),
        scratch_shapes=[pltpu.VMEM((1, LANES), F32)],
        compiler_params=_params("parallel", "arbitrary"),
        name="qkv_prompt",
    )(x, g, w, w, w, w, b_f, q_gain, k_gain)


def _rglru_kernel(xa_ref, ga_ref, cprev_ref, h0_ref, cw_ref, cb_ref, wr_ref, br_ref,
                  wi_ref, bi_ref, lam_ref, ya_ref, cnew_ref, hlast_ref,
                  xpad_ref, a_ref, u_ref, hs_ref, hc_ref, *, n_blocks):
    t = pl.program_id(1)
    tc = xa_ref.shape[0]
    pad = SUBLANES

    @pl.when(t == 0)
    def _():
        xpad_ref[pad - (CONV_WIDTH - 1):pad, :] = cprev_ref[...]
        hc_ref[...] = h0_ref[...]

    xa = xa_ref[...]
    xpad_ref[pad:pad + tc, :] = xa
    xc = cb_ref[...] + cw_ref[CONV_WIDTH - 1:CONV_WIDTH, :] * xa
    for jj in range(CONV_WIDTH - 1):
        off = pad - (CONV_WIDTH - 1) + jj
        xc = xc + cw_ref[jj:jj + 1, :] * xpad_ref[off:off + tc, :]
    tail = xa[tc - (CONV_WIDTH - 1):tc, :]
    xpad_ref[pad - (CONV_WIDTH - 1):pad, :] = tail
    cnew_ref[...] = tail

    lam = lam_ref[...]
    softplus_neg_lam = jnp.maximum(-lam, 0.0) + jnp.log1p(jnp.exp(-jnp.abs(lam)))
    for nb in range(n_blocks):
        sl = slice(nb * LANES, (nb + 1) * LANES)
        xb = xc[:, sl]
        xb16 = xb.astype(BF16)
        r = _sigmoid(jnp.dot(xb16, wr_ref[nb], preferred_element_type=F32) + br_ref[:, sl])
        ig = _sigmoid(jnp.dot(xb16, wi_ref[nb], preferred_element_type=F32) + bi_ref[:, sl])
        log_a = (-LRU_C) * r * softplus_neg_lam[:, sl]
        a = jnp.exp(log_a)
        a_ref[:, sl] = a
        u_ref[:, sl] = jnp.sqrt(jnp.tanh(-log_a) * (1.0 + a * a)) * (ig * xb)

    def step(g, h):
        base = pl.multiple_of(g * SUBLANES, SUBLANES)
        for r_ in range(SUBLANES):
            h = a_ref[pl.ds(base + r_, 1), :] * h + u_ref[pl.ds(base + r_, 1), :]
            hs_ref[pl.ds(base + r_, 1), :] = h
        return h

    h_fin = lax.fori_loop(0, tc // SUBLANES, step, hc_ref[...])
    hc_ref[...] = h_fin
    hlast_ref[...] = h_fin
    ya_ref[...] = (hs_ref[...] * jax.nn.gelu(ga_ref[...])).astype(BF16)


def _rglru(z, conv_prev, h0, conv_w, conv_b, w_r, b_r, w_i, b_i, lam, layer, *, tc):
    b, t, _ = z.shape
    d_rnn = conv_w.shape[-1]
    n_blocks = w_r.shape[1]
    vec = lambda: pl.BlockSpec((None, 1, d_rnn), lambda bi, ti: (layer, 0, 0))
    gate_w = lambda: pl.BlockSpec((None, n_blocks, LANES, LANES), lambda bi, ti: (layer, 0, 0, 0))
    return pl.pallas_call(
        functools.partial(_rglru_kernel, n_blocks=n_blocks),
        out_shape=(jax.ShapeDtypeStruct((b, t, d_rnn), BF16),
                   jax.ShapeDtypeStruct((b, CONV_WIDTH - 1, d_rnn), F32),
                   jax.ShapeDtypeStruct((b, 1, d_rnn), F32)),
        grid=(b, t // tc),
        in_specs=[
            pl.BlockSpec((None, tc, d_rnn), lambda bi, ti: (bi, ti, 0)),
            pl.BlockSpec((None, tc, d_rnn), lambda bi, ti: (bi, ti, 1)),
            pl.BlockSpec((None, CONV_WIDTH - 1, d_rnn), lambda bi, ti: (bi, 0, 0)),
            pl.BlockSpec((None, 1, d_rnn), lambda bi, ti: (bi, 0, 0)),
            pl.BlockSpec((None, CONV_WIDTH, d_rnn), lambda bi, ti: (layer, 0, 0)),
            vec(), gate_w(), vec(), gate_w(), vec(), vec(),
        ],
        out_specs=(pl.BlockSpec((None, tc, d_rnn), lambda bi, ti: (bi, ti, 0)),
                   pl.BlockSpec((None, CONV_WIDTH - 1, d_rnn), lambda bi, ti: (bi, 0, 0)),
                   pl.BlockSpec((None, 1, d_rnn), lambda bi, ti: (bi, 0, 0))),
        scratch_shapes=[pltpu.VMEM((tc + SUBLANES, d_rnn), F32), pltpu.VMEM((tc, d_rnn), F32),
                        pltpu.VMEM((tc, d_rnn), F32), pltpu.VMEM((tc, d_rnn), F32),
                        pltpu.VMEM((1, d_rnn), F32)],
        compiler_params=_params("parallel", "arbitrary"),
        name="rglru",
    )(z, z, conv_prev, h0, conv_w, conv_b, w_r, b_r, w_i, b_i, lam)


def _attn_prompt_kernel(q_ref, k_ref, vt_ref, o_ref, s0_ref, s1_ref, m_ref, l_ref, acc_ref, *, cw):
    qi = pl.program_id(2)
    tq = q_ref.shape[0]
    tk = vt_ref.shape[-1]
    groups = range(tq // cw)

    m_ref[...] = jnp.full(m_ref.shape, NEG_BIG, F32)
    l_ref[...] = jnp.zeros(l_ref.shape, F32)
    acc_ref[...] = jnp.zeros(acc_ref.shape, F32)

    def qk(jb, s_ref, grps):
        k = k_ref[pl.ds(pl.multiple_of(jb * tk, tk), tk), :]
        for c in grps:
            sl = slice(c * cw, (c + 1) * cw)
            s_ref[:, sl] = lax.dot_general(k, q_ref[sl, :], (((1,), (1,)), ((), ())),
                                           preferred_element_type=F32)

    def softmax_pv(jb, s_ref, grps, k_lo):
        vt = vt_ref[jb]
        for c in grps:
            sl = slice(c * cw, (c + 1) * cw)
            s = s_ref[:, sl]
            if k_lo is not None and k_lo + tk - 1 > c * cw:
                key = lax.broadcasted_iota(jnp.int32, (tk, cw), 0) + (k_lo - c * cw)
                qry = lax.broadcasted_iota(jnp.int32, (tk, cw), 1)
                s = jnp.where(key <= qry, s, NEG_BIG)
            m_prev = m_ref[:, sl]
            m_new = jnp.maximum(m_prev, jnp.max(s, axis=0, keepdims=True))
            alpha = jnp.exp2(m_prev - m_new)
            p = jnp.exp2(s - m_new)
            l_ref[:, sl] = alpha * l_ref[:, sl] + jnp.sum(p, axis=0, keepdims=True)
            acc_ref[:, sl] = alpha * acc_ref[:, sl] + jnp.dot(vt, p.astype(BF16),
                                                              preferred_element_type=F32)
            m_ref[:, sl] = m_new

    qk(0, s0_ref, groups)

    def body(i, carry):
        qk(2 * i + 1, s1_ref, groups)
        softmax_pv(2 * i, s0_ref, groups, None)
        qk(2 * i + 2, s0_ref, groups)
        softmax_pv(2 * i + 1, s1_ref, groups, None)
        return carry

    lax.fori_loop(0, qi, body, 0)

    late = [c for c in groups if c * cw + cw - 1 >= tk]
    qk(2 * qi + 1, s1_ref, late)
    softmax_pv(2 * qi, s0_ref, groups, 0)
    softmax_pv(2 * qi + 1, s1_ref, late, tk)

    o_ref[...] = (acc_ref[...] / l_ref[...]).T.astype(o_ref.dtype)


def _attn_prompt(qp, kp, vt, *, tq, cw):
    b, h, s, dq = qp.shape
    _, _, nblk, hd, tk = vt.shape
    assert tq == 2 * tk and tk % cw == 0
    return pl.pallas_call(
        functools.partial(_attn_prompt_kernel, cw=cw),
        out_shape=jax.ShapeDtypeStruct((b, s, h * hd), BF16),
        grid=(b, h, s // tq),
        in_specs=[
            pl.BlockSpec((None, None, tq, dq), lambda bi, hi, qi: (bi, hi, qi, 0)),
            pl.BlockSpec((None, None, s, dq), lambda bi, hi, qi: (bi, hi, 0, 0)),
            pl.BlockSpec((None, None, nblk, hd, tk), lambda bi, hi, qi: (bi, hi, 0, 0, 0)),
        ],
        out_specs=pl.BlockSpec((None, tq, hd), lambda bi, hi, qi: (bi, qi, hi)),
        scratch_shapes=[pltpu.VMEM((tk, tq), F32), pltpu.VMEM((tk, tq), F32),
                        pltpu.VMEM((1, tq), F32), pltpu.VMEM((1, tq), F32),
                        pltpu.VMEM((hd, tq), F32)],
        compiler_params=_params("parallel", "parallel", "arbitrary"),
        name="attn_prompt",
    )(qp, kp, vt)


def _cumsum_lanes_kernel(x_ref, o_ref):
    rows, n = x_ref.shape
    r_i = lax.broadcasted_iota(jnp.int32, (LANES, LANES), 0)
    c_i = lax.broadcasted_iota(jnp.int32, (LANES, LANES), 1)
    upper = jnp.where(r_i <= c_i, 1.0, 0.0).astype(BF16)
    carry = jnp.zeros((rows, 1), F32)
    for ch in range(n // LANES):
        sl = slice(ch * LANES, (ch + 1) * LANES)
        cs = carry
        for piece in _split3(x_ref[:, sl]):
            cs = cs + jnp.dot(piece.astype(BF16), upper, preferred_element_type=F32)
        o_ref[:, sl] = cs
        carry = cs[:, LANES - 1:LANES]


def _cumsum_lanes(x, *, tr):
    rows, n = x.shape
    return pl.pallas_call(
        _cumsum_lanes_kernel,
        out_shape=jax.ShapeDtypeStruct((rows, n), F32),
        grid=(rows // tr,),
        in_specs=[pl.BlockSpec((tr, n), lambda i: (i, 0))],
        out_specs=pl.BlockSpec((tr, n), lambda i: (i, 0)),
        compiler_params=_params("parallel"),
        name="cumsum_lanes",
    )(x)


def _attn_sample_kernel(zq_ref, zk_ref, zv_ref, fl_ref, bf_ref, qg_ref, kg_ref,
                        kc_ref, vc_ref, cp_ref, yb_ref, ko_ref, vo_ref, lf_ref, *, scale):
    h = pl.program_id(1)
    t = zq_ref.shape[0]
    p_len = kc_ref.shape[0]
    lane = lax.broadcasted_iota(jnp.int32, (t, LANES), 1)

    logf_all = _log_sigmoid(fl_ref[...] + bf_ref[...])
    lf_ref[...] = logf_all
    logf = jnp.sum(jnp.where(lane == h, logf_all, 0.0), axis=1, keepdims=True)
    cl = _cumsum_rows(jnp.broadcast_to(logf, (t, LANES)))

    qn = _rms(zq_ref[...], qg_ref[...]) * scale
    kn = _rms(zk_ref[...], kg_ref[...])
    v = zv_ref[...]
    ko_ref[...] = kn
    vo_ref[...] = v

    cp = cp_ref[...]
    kc = kc_ref[...].astype(BF16)
    s_p = lax.dot_general(qn.astype(BF16), kc, (((1,), (1,)), ((), ())), preferred_element_type=F32)
    s_p = s_p + cl[:, 0:1] + (cp[:, p_len - 1:p_len] - cp)

    q_aug = jnp.concatenate([qn, _aug_q(cl, lane)], axis=1).astype(BF16)
    k_aug = jnp.concatenate([kn, _aug_k(cl, lane)], axis=1).astype(BF16)
    k_pad = jnp.concatenate([k_aug, jnp.zeros((LANES - t, 2 * LANES), BF16)], axis=0)
    v_pad = jnp.concatenate([v.astype(BF16), jnp.zeros((LANES - t, LANES), BF16)], axis=0)
    s_n = lax.dot_general(q_aug, k_pad, (((1,), (1,)), ((), ())), preferred_element_type=F32)
    row = lax.broadcasted_iota(jnp.int32, (t, LANES), 0)
    s_n = jnp.where(lane <= row, s_n, NEG_BIG)

    m = jnp.maximum(jnp.max(s_p, axis=1, keepdims=True), jnp.max(s_n, axis=1, keepdims=True))
    p_p = jnp.exp(s_p - m)
    p_n = jnp.exp(s_n - m)
    l = jnp.sum(p_p, axis=1, keepdims=True) + jnp.sum(p_n, axis=1, keepdims=True)
    o = jnp.dot(p_p.astype(BF16), vc_ref[...].astype(BF16), preferred_element_type=F32)
    o = o + jnp.dot(p_n.astype(BF16), v_pad, preferred_element_type=F32)
    yb_ref[...] = (o / l).astype(yb_ref.dtype)


def _attn_sample(z, b_f, q_gain, k_gain, cache_k, cache_v, c_past, layer, *, n_heads, col0):
    b, t, _ = z.shape
    hd = LANES
    p_len = cache_k.shape[-2]
    scale = float(hd) ** -0.5
    zspec = lambda sec: pl.BlockSpec((None, t, hd), lambda bi, hi: (bi, 0, col0 + sec * n_heads + hi))
    cache = lambda: pl.BlockSpec((None, None, None, p_len, hd), lambda bi, hi: (layer, bi, hi, 0, 0))
    return pl.pallas_call(
        functools.partial(_attn_sample_kernel, scale=scale),
        out_shape=(jax.ShapeDtypeStruct((b, t, n_heads * hd), BF16),
                   jax.ShapeDtypeStruct((b, n_heads, t, hd), F32),
                   jax.ShapeDtypeStruct((b, n_heads, t, hd), F32),
                   jax.ShapeDtypeStruct((b, t, LANES), F32)),
        grid=(b, n_heads),
        in_specs=[
            zspec(0), zspec(1), zspec(2),
            pl.BlockSpec((None, t, LANES), lambda bi, hi: (bi, 0, col0 + 3 * n_heads)),
            pl.BlockSpec((None, 1, LANES), lambda bi, hi: (layer, 0, 0)),
            pl.BlockSpec((None, 1, hd), lambda bi, hi: (layer, 0, 0)),
            pl.BlockSpec((None, 1, hd), lambda bi, hi: (layer, 0, 0)),
            cache(), cache(),
            pl.BlockSpec((None, None, None, 1, p_len), lambda bi, hi: (layer, bi, hi, 0, 0)),
        ],
        out_specs=(pl.BlockSpec((None, t, hd), lambda bi, hi: (bi, 0, hi)),
                   pl.BlockSpec((None, None, t, hd), lambda bi, hi: (bi, hi, 0, 0)),
                   pl.BlockSpec((None, None, t, hd), lambda bi, hi: (bi, hi, 0, 0)),
                   pl.BlockSpec((None, t, LANES), lambda bi, hi: (bi, 0, 0))),
        compiler_params=_params("parallel", "arbitrary"),
        name="attn_sample",
    )(z, z, z, z, b_f, q_gain, k_gain, cache_k, cache_v, c_past)


def _merge_kernel(ya_ref, yb_ref, ga_ref, gb_ref, wa_ref, wb_ref, o_ref, *, cn):
    ya, yb = ya_ref[...], yb_ref[...]
    for j0 in range(0, o_ref.shape[-1], cn):
        sl = slice(j0, j0 + cn)
        a = jnp.dot(ya, wa_ref[:, sl], preferred_element_type=F32)
        b = jnp.dot(yb, wb_ref[:, sl], preferred_element_type=F32)
        o_ref[:, sl] = (_sigmoid(ga_ref[:, sl]) * a + _sigmoid(gb_ref[:, sl]) * b).astype(o_ref.dtype)


def _resident(shape, index_map):
    return pl.BlockSpec(shape, index_map, pipeline_mode=pl.Buffered(1))


def _merge(ya, yb, z, w_a, w_b, layer, *, gate_col, tm):
    n, d_a = ya.shape
    d_b = yb.shape[-1]
    d = w_a.shape[-1]
    ga0 = gate_col // d
    return pl.pallas_call(
        functools.partial(_merge_kernel, cn=_tile(d, 1024)),
        out_shape=jax.ShapeDtypeStruct((n, d), BF16),
        grid=(n // tm,),
        in_specs=[
            pl.BlockSpec((tm, d_a), lambda i: (i, 0)),
            pl.BlockSpec((tm, d_b), lambda i: (i, 0)),
            pl.BlockSpec((tm, d), lambda i: (i, ga0)),
            pl.BlockSpec((tm, d), lambda i: (i, ga0 + 1)),
            _resident((None, d_a, d), lambda i: (layer, 0, 0)),
            _resident((None, d_b, d), lambda i: (layer, 0, 0)),
        ],
        out_specs=pl.BlockSpec((tm, d), lambda i: (i, 0)),
        compiler_params=_params("parallel"),
        name="merge",
    )(ya, yb, z, z, w_a, w_b)


def _matmul_res_kernel(m_ref, w_ref, x_ref, o_ref):
    o_ref[...] = x_ref[...] + jnp.dot(m_ref[...], w_ref[...], preferred_element_type=F32)


def _matmul_res(m, w, x, layer, *, tm):
    n, k = m.shape
    d = w.shape[-1]
    return pl.pallas_call(
        _matmul_res_kernel,
        out_shape=jax.ShapeDtypeStruct((n, d), F32),
        grid=(n // tm,),
        in_specs=[
            pl.BlockSpec((tm, k), lambda i: (i, 0)),
            _resident((None, k, d), lambda i: (layer, 0, 0)),
            pl.BlockSpec((tm, d), lambda i: (i, 0)),
        ],
        out_specs=pl.BlockSpec((tm, d), lambda i: (i, 0)),
        compiler_params=_params("parallel"),
        name="matmul_res",
    )(m, w, x)


def _mlp_kernel(x_ref, g_ref, wu_ref, wd_ref, o_ref, xn_ref, acc_ref):
    f = pl.program_id(1)

    @pl.when(f == 0)
    def _():
        x = x_ref[...]
        xn_ref[...] = _rms(x, g_ref[...]).astype(BF16)
        acc_ref[...] = x

    hid = jnp.maximum(jnp.dot(xn_ref[...], wu_ref[...], preferred_element_type=F32), 0.0)
    acc_ref[...] += jnp.dot((hid * hid).astype(BF16), wd_ref[...], preferred_element_type=F32)

    @pl.when(f == pl.num_programs(1) - 1)
    def _():
        o_ref[...] = acc_ref[...]


def _mlp(x, g, w_up, w_down, layer, *, tm, tf):
    n, d = x.shape
    d_ff = w_up.shape[-1]
    return pl.pallas_call(
        _mlp_kernel,
        out_shape=jax.ShapeDtypeStruct((n, d), F32),
        grid=(n // tm, d_ff // tf),
        in_specs=[
            pl.BlockSpec((tm, d), lambda i, f: (i, 0)),
            pl.BlockSpec((None, 1, d), lambda i, f: (layer, 0, 0)),
            pl.BlockSpec((None, d, tf), lambda i, f: (layer, 0, f)),
            pl.BlockSpec((None, tf, d), lambda i, f: (layer, f, 0)),
        ],
        out_specs=pl.BlockSpec((tm, d), lambda i, f: (i, 0)),
        scratch_shapes=[pltpu.VMEM((tm, d), BF16), pltpu.VMEM((tm, d), F32)],
        compiler_params=_params("parallel", "arbitrary"),
        name="mlp",
    )(x, g, w_up, w_down)


def _ple_kernel(x_ref, g_ref, wg_ref, p_ref, we_ref, o_ref, *, cn):
    xn = _rms(x_ref[...], g_ref[...]).astype(BF16)
    pe = p_ref[...].astype(BF16)
    for j0 in range(0, o_ref.shape[-1], cn):
        sl = slice(j0, j0 + cn)
        gate = _sigmoid(jnp.dot(xn, wg_ref[:, sl], preferred_element_type=F32))
        emb = jnp.dot(pe, we_ref[:, sl], preferred_element_type=F32)
        o_ref[:, sl] = x_ref[:, sl] + gate * emb


def _ple(x, g, w_pg, p, w_pe, layer, *, tm):
    n, d = x.shape
    d_ple = p.shape[-1]
    return pl.pallas_call(
        functools.partial(_ple_kernel, cn=_tile(d, 1024)),
        out_shape=jax.ShapeDtypeStruct((n, d), F32),
        grid=(n // tm,),
        in_specs=[
            pl.BlockSpec((tm, d), lambda i: (i, 0)),
            pl.BlockSpec((None, 1, d), lambda i: (layer, 0, 0)),
            _resident((None, d, d), lambda i: (layer, 0, 0)),
            pl.BlockSpec((None, tm, d_ple), lambda i: (layer, i, 0)),
            _resident((None, d_ple, d), lambda i: (layer, 0, 0)),
        ],
        out_specs=pl.BlockSpec((tm, d), lambda i: (i, 0)),
        compiler_params=_params("parallel"),
        name="ple",
    )(x, g, w_pg, p, w_pe)


def _tile(n, pref):
    t = min(n, pref)
    while n % t:
        t //= 2
    return t


def kernel(x_prompt, x_sample, cache_k, cache_v, cache_logf, state_conv, state_lru, p_prompt, p_sample, norm_mix, norm_mlp, norm_ple, w_in, b_f, conv_w, conv_b, w_rg, b_rg, w_ig, b_ig, lru_lambda, q_gain, k_gain, w_a_out, w_b_out, w_o, w_up, w_down, w_pe, w_pg):
    depth, d, _ = w_in.shape
    bp, sp, _ = x_prompt.shape
    bs, ts, _ = x_sample.shape
    d_rnn = conv_w.shape[-1]
    n_heads = b_f.shape[-1]
    hd = q_gain.shape[-1]
    d_attn = n_heads * hd
    d_ple = w_pe.shape[1]
    p_len = cache_k.shape[-2]
    assert hd == LANES and w_rg.shape[-1] == LANES and n_heads <= LANES

    o_q = 2 * d_rnn
    o_fl = o_q + 3 * d_attn
    o_gate = o_fl + n_heads
    fl_pad = _tile(d_attn, 1024)
    w_fl = jnp.pad(w_in[:, :, o_fl:o_gate], ((0, 0), (0, 0), (0, fl_pad - n_heads)))
    w_all = jnp.concatenate([w_in[:, :, :o_q], w_in[:, :, o_gate:], w_in[:, :, o_q:o_fl], w_fl],
                            axis=-1).astype(BF16)
    gate_col = o_q
    n_ag = o_q + 2 * d
    qkv_tile0 = n_ag // LANES

    row = lambda a: a.reshape(depth, 1, a.shape[-1])
    bf_pad = jnp.pad(b_f, ((0, 0), (0, LANES - n_heads))).reshape(depth, 1, LANES)
    g_mix, g_mlp, g_ple = row(norm_mix), row(norm_mlp), row(norm_ple)
    qg, kg = row(q_gain), row(k_gain)
    cb, br, bi, lam = row(conv_b), row(b_rg), row(b_ig), row(lru_lambda)
    w_r16, w_i16 = w_rg.astype(BF16), w_ig.astype(BF16)
    w_a16, w_b16, w_o16 = w_a_out.astype(BF16), w_b_out.astype(BF16), w_o.astype(BF16)
    w_up16, w_dn16 = w_up.astype(BF16), w_down.astype(BF16)
    w_pe16, w_pg16 = w_pe.astype(BF16), w_pg.astype(BF16)

    n_p, n_s = bp * sp, bs * ts
    pp = p_prompt.reshape(depth, n_p, d_ple)
    ps = p_sample.reshape(depth, n_s, d_ple)

    rows = depth * bs * n_heads
    c_past = _cumsum_lanes(cache_logf.reshape(rows, p_len), tr=_tile(rows, 256))
    c_past = c_past.reshape(depth, bs, n_heads, 1, p_len)

    tn = _tile(d, 1024)
    tf = _tile(w_up.shape[-1], 1024)

    def tokenwise(x, y_a, y_b, z, pe, layer):
        tm = _tile(x.shape[0], 512)
        merged = _merge(y_a, y_b, z, w_a16, w_b16, layer, gate_col=gate_col, tm=tm)
        x = _matmul_res(merged, w_o16, x, layer, tm=tm)
        x = _mlp(x, g_mlp, w_up16, w_dn16, layer, tm=tm, tf=tf)
        return _ple(x, g_ple, w_pg16, pe, w_pe16, layer, tm=tm)

    xp = x_prompt.reshape(n_p, d)
    xs = x_sample.reshape(n_s, d)
    tm_p = _tile(n_p, 1024)
    tm_s = _tile(n_s, 1024)
    tq = _tile(sp, 512)
    conv0 = jnp.zeros((bp, CONV_WIDTH - 1, d_rnn), F32)
    h00 = jnp.zeros((bp, 1, d_rnn), F32)
    outs = [[] for _ in range(10)]
    for layer in range(depth):
        z = _normed_matmul(xp, g_mix, w_all, layer, n_out=n_ag, tm=tm_p, tn=tn)
        qp, kp, vt, k1, v1, lf1 = _qkv_prompt(xp.reshape(bp, sp, d), g_mix, w_all, bf_pad, qg, kg,
                                              layer, n_heads=n_heads, col0=qkv_tile0, tm=tq)
        y_a, c1, l1 = _rglru(z.reshape(bp, sp, -1), conv0, h00, conv_w, cb, w_r16, br, w_i16, bi, lam,
                             layer, tc=_tile(sp, 256))
        y_b = _attn_prompt(qp, kp, vt, tq=_tile(sp, 2 * tq), cw=_tile(tq, 256))
        xp = tokenwise(xp, y_a.reshape(n_p, d_rnn), y_b.reshape(n_p, d_attn), z, pp, layer)
        f1 = lf1[:, :, :n_heads].transpose(0, 2, 1)

        zs = _normed_matmul(xs, g_mix, w_all, layer, n_out=w_all.shape[-1], tm=tm_s, tn=tn)
        zs3 = zs.reshape(bs, ts, -1)
        y_a, c2, l2 = _rglru(zs3, state_conv[layer], state_lru[layer].reshape(bs, 1, d_rnn), conv_w, cb,
                             w_r16, br, w_i16, bi, lam, layer, tc=ts)
        y_b, k2, v2, lf2 = _attn_sample(zs3, bf_pad, qg, kg, cache_k, cache_v, c_past, layer,
                                        n_heads=n_heads, col0=qkv_tile0)
        xs = tokenwise(xs, y_a.reshape(n_s, d_rnn), y_b.reshape(n_s, d_attn), zs, ps, layer)
        f2 = lf2[:, :, :n_heads].transpose(0, 2, 1)

        for lst, val in zip(outs, (k1, v1, f1, c1, l1.reshape(bp, d_rnn),
                                   k2, v2, f2, c2, l2.reshape(bs, d_rnn))):
            lst.append(val)

    return (xp.reshape(bp, sp, d), xs.reshape(bs, ts, d)) + tuple(jnp.stack(o) for o in outs)
```

```python
import functools

import jax
import jax.numpy as jnp
from jax import lax
from jax.experimental import pallas as pl
from jax.experimental.pallas import tpu as pltpu

F32 = jnp.float32
BF16 = jnp.bfloat16

EPS = 1e-6
LRU_C = 8.0
CONV_WIDTH = 4
LANES = 128
SUBLANES = 8
BF16_ROWS = 16
AUG = 128
NEG_BIG = -1e30
LOG2_E = 1.4426950408889634
VMEM_LIMIT = 56 * 1024 * 1024


def _params(*sem):
    return pltpu.CompilerParams(dimension_semantics=sem, vmem_limit_bytes=VMEM_LIMIT)


def _rms(x, g):
    return x * lax.rsqrt(jnp.mean(x * x, axis=-1, keepdims=True) + EPS) * g


def _log_sigmoid(x):
    return jnp.minimum(x, 0.0) - jnp.log1p(jnp.exp(-jnp.abs(x)))


def _sigmoid(x):
    return 1.0 / (1.0 + jnp.exp(-x))


def _split3(c):
    hi = c.astype(BF16).astype(F32)
    r = c - hi
    mid = r.astype(BF16).astype(F32)
    lo = (r - mid).astype(BF16).astype(F32)
    return hi, mid, lo


def _aug_q(c, lane):
    hi, mid, lo = _split3(c)
    return jnp.where(lane == 0, hi, jnp.where(lane == 1, mid, jnp.where(
        lane == 2, lo, jnp.where(lane < 6, 1.0, 0.0))))


def _aug_k(c, lane):
    hi, mid, lo = _split3(c)
    return jnp.where(lane < 3, 1.0, jnp.where(lane == 3, -hi, jnp.where(
        lane == 4, -mid, jnp.where(lane == 5, -lo, 0.0))))


def _cumsum_rows(x):
    n = x.shape[0]
    row = lax.broadcasted_iota(jnp.int32, x.shape, 0)
    d = 1
    while d < n:
        x = x + jnp.where(row >= d, pltpu.roll(x, d, 0), 0.0)
        d *= 2
    return x


def _normed_matmul_kernel(*refs, starts):
    x_ref, g_ref = refs[:2]
    w_refs = refs[2:2 + len(starts) - 1]
    o_ref, xn_ref = refs[-2:]
    j = pl.program_id(1)

    @pl.when(j == 0)
    def _():
        xn_ref[...] = _rms(x_ref[...], g_ref[...]).astype(BF16)

    for s, w_ref in enumerate(w_refs):
        @pl.when((j >= starts[s]) & (j < starts[s + 1]))
        def _(w_ref=w_ref):
            o_ref[...] = jnp.dot(xn_ref[...], w_ref[...], preferred_element_type=F32)


def _normed_matmul(x, g, ws, layer, *, tm, tn):
    n, d = x.shape
    starts = [0]
    for w in ws:
        starts.append(starts[-1] + w.shape[-1] // tn)

    def w_spec(s):
        lo, nb = starts[s], starts[s + 1] - starts[s]
        return pl.BlockSpec((None, d, tn), lambda i, j: (layer, 0, jnp.clip(j - lo, 0, nb - 1)))

    return pl.pallas_call(
        functools.partial(_normed_matmul_kernel, starts=tuple(starts)),
        out_shape=jax.ShapeDtypeStruct((n, starts[-1] * tn), F32),
        grid=(n // tm, starts[-1]),
        in_specs=[
            pl.BlockSpec((tm, d), lambda i, j: (i, 0)),
            pl.BlockSpec((None, 1, d), lambda i, j: (layer, 0, 0)),
        ] + [w_spec(s) for s in range(len(ws))],
        out_specs=pl.BlockSpec((tm, tn), lambda i, j: (i, j)),
        scratch_shapes=[pltpu.VMEM((tm, d), BF16)],
        compiler_params=_params("parallel", "arbitrary"),
        name="normed_matmul",
    )(x, g, *ws)


def _qkv_prompt_kernel(x_ref, g_ref, wq_ref, wk_ref, wv_ref, wfl_ref, bf_ref, qg_ref, kg_ref,
                       qt_ref, kp_ref, vt_ref, ko_ref, vo_ref, lf_ref, carry_ref, *, n_heads, scale):
    tm = x_ref.shape[0]
    xn = _rms(x_ref[...], g_ref[...]).astype(BF16)
    logf = _log_sigmoid(jnp.dot(xn, wfl_ref[...], preferred_element_type=F32) + bf_ref[...])
    lf_ref[...] = logf

    @pl.when(pl.program_id(1) == 0)
    def _():
        carry_ref[...] = jnp.zeros_like(carry_ref)

    c = _cumsum_rows(logf) + carry_ref[...]
    carry_ref[...] = c[tm - 1:tm, :]
    hi, mid, lo = _split3(c * LOG2_E)
    lane = lax.broadcasted_iota(jnp.int32, (tm, LANES), 1)

    def to_lane(x, dst, h):
        return pltpu.roll(x, (dst - h) % LANES, 1)

    hi_t, mid_t, lo_t = hi.T, mid.T, lo.T
    row = lax.broadcasted_iota(jnp.int32, (BF16_ROWS, tm), 0)
    qg = qg_ref[...] * (scale * LOG2_E)
    zq = jnp.dot(xn, wq_ref[...], preferred_element_type=F32)
    for h in range(n_heads):
        aug = jnp.where(row == 0, hi_t[h:h + 1], jnp.where(row == 1, mid_t[h:h + 1], jnp.where(
            row == 2, lo_t[h:h + 1], jnp.where(row < 6, 1.0, 0.0))))
        qt_ref[h, 0:LANES, :] = _rms(zq[:, h * LANES:(h + 1) * LANES], qg).T.astype(BF16)
        qt_ref[h, LANES:LANES + BF16_ROWS, :] = aug.astype(BF16)
        qt_ref[h, LANES + BF16_ROWS:LANES + AUG, :] = jnp.zeros((AUG - BF16_ROWS, tm), BF16)

    zk = jnp.dot(xn, wk_ref[...], preferred_element_type=F32)
    for h in range(n_heads):
        aug = jnp.where(lane < 3, 1.0, jnp.where(lane == 3, -to_lane(hi, 3, h), jnp.where(
            lane == 4, -to_lane(mid, 4, h), jnp.where(lane == 5, -to_lane(lo, 5, h), 0.0))))
        kn = _rms(zk[:, h * LANES:(h + 1) * LANES], kg_ref[...])
        ko_ref[h] = kn
        kp_ref[h, :, 0:LANES] = kn.astype(BF16)
        kp_ref[h, :, LANES:LANES + AUG] = aug.astype(BF16)

    ones_row = jnp.where(row == 0, 1.0, 0.0).astype(BF16)
    zv = jnp.dot(xn, wv_ref[...], preferred_element_type=F32)
    for h in range(n_heads):
        zh = zv[:, h * LANES:(h + 1) * LANES]
        vo_ref[h] = zh
        vt_ref[h, 0:LANES, :] = zh.T.astype(BF16)
        vt_ref[h, LANES:LANES + BF16_ROWS, :] = ones_row


def _qkv_prompt(x, g, w_qkv, w_fl, b_f, q_gain, k_gain, layer, *, n_heads, tm):
    b, s, d = x.shape
    hd = LANES
    d_attn = n_heads * hd
    scale = float(hd) ** -0.5
    head_major = lambda width, dt: jax.ShapeDtypeStruct((b, n_heads, s, width), dt)
    head_spec = lambda width: pl.BlockSpec((None, n_heads, tm, width), lambda bi, i: (bi, 0, i, 0))
    resident = lambda shape, idx: pl.BlockSpec(shape, idx, pipeline_mode=pl.Buffered(1))
    wsec = lambda sec: resident((None, d, d_attn), lambda bi, i: (layer, 0, sec))
    return pl.pallas_call(
        functools.partial(_qkv_prompt_kernel, n_heads=n_heads, scale=scale),
        out_shape=(jax.ShapeDtypeStruct((b, n_heads, hd + AUG, s), BF16), head_major(hd + AUG, BF16),
                   jax.ShapeDtypeStruct((b, n_heads, s // tm, hd + BF16_ROWS, tm), BF16),
                   head_major(hd, F32), head_major(hd, F32),
                   jax.ShapeDtypeStruct((b, s, LANES), F32)),
        grid=(b, s // tm),
        in_specs=[
            pl.BlockSpec((None, tm, d), lambda bi, i: (bi, i, 0)),
            pl.BlockSpec((None, 1, d), lambda bi, i: (layer, 0, 0)),
            wsec(0), wsec(1), wsec(2),
            resident((None, d, LANES), lambda bi, i: (layer, 0, 0)),
            pl.BlockSpec((None, 1, LANES), lambda bi, i: (layer, 0, 0)),
            pl.BlockSpec((None, 1, hd), lambda bi, i: (layer, 0, 0)),
            pl.BlockSpec((None, 1, hd), lambda bi, i: (layer, 0, 0)),
        ],
        out_specs=(pl.BlockSpec((None, n_heads, hd + AUG, tm), lambda bi, i: (bi, 0, 0, i)),
                   head_spec(hd + AUG),
                   pl.BlockSpec((None, n_heads, None, hd + BF16_ROWS, tm), lambda bi, i: (bi, 0, i, 0, 0)),
                   head_spec(hd), head_spec(hd),
                   pl.BlockSpec((None, tm, LANES), lambda bi, i: (bi, i, 0))),
        scratch_shapes=[pltpu.VMEM((1, LANES), F32)],
        compiler_params=_params("parallel", "arbitrary"),
        name="qkv_prompt",
    )(x, g, w_qkv, w_qkv, w_qkv, w_fl, b_f, q_gain, k_gain)


def _rglru_kernel(xa_ref, ga_ref, cprev_ref, h0_ref, cw_ref, cb_ref, wr_ref, br_ref,
                  wi_ref, bi_ref, lam_ref, ya_ref, cnew_ref, hlast_ref,
                  xpad_ref, a_ref, u_ref, hs_ref, hc_ref, *, n_blocks):
    t = pl.program_id(1)
    tc = xa_ref.shape[0]
    pad = SUBLANES

    @pl.when(t == 0)
    def _():
        xpad_ref[pad - (CONV_WIDTH - 1):pad, :] = cprev_ref[...]
        hc_ref[...] = h0_ref[...]

    xa = xa_ref[...]
    xpad_ref[pad:pad + tc, :] = xa
    xc = cb_ref[...] + cw_ref[CONV_WIDTH - 1:CONV_WIDTH, :] * xa
    for jj in range(CONV_WIDTH - 1):
        off = pad - (CONV_WIDTH - 1) + jj
        xc = xc + cw_ref[jj:jj + 1, :] * xpad_ref[off:off + tc, :]
    tail = xa[tc - (CONV_WIDTH - 1):tc, :]
    xpad_ref[pad - (CONV_WIDTH - 1):pad, :] = tail
    cnew_ref[...] = tail

    lam = lam_ref[...]
    softplus_neg_lam = jnp.maximum(-lam, 0.0) + jnp.log1p(jnp.exp(-jnp.abs(lam)))
    for nb in range(n_blocks):
        sl = slice(nb * LANES, (nb + 1) * LANES)
        xb = xc[:, sl]
        xb16 = xb.astype(BF16)
        r = _sigmoid(jnp.dot(xb16, wr_ref[nb], preferred_element_type=F32) + br_ref[:, sl])
        ig = _sigmoid(jnp.dot(xb16, wi_ref[nb], preferred_element_type=F32) + bi_ref[:, sl])
        log_a = (-LRU_C) * r * softplus_neg_lam[:, sl]
        a = jnp.exp(log_a)
        a_ref[:, sl] = a
        u_ref[:, sl] = jnp.sqrt(jnp.tanh(-log_a) * (1.0 + a * a)) * (ig * xb)

    def step(g, h):
        base = pl.multiple_of(g * SUBLANES, SUBLANES)
        for r_ in range(SUBLANES):
            h = a_ref[pl.ds(base + r_, 1), :] * h + u_ref[pl.ds(base + r_, 1), :]
            hs_ref[pl.ds(base + r_, 1), :] = h
        return h

    h_fin = lax.fori_loop(0, tc // SUBLANES, step, hc_ref[...])
    hc_ref[...] = h_fin
    hlast_ref[...] = h_fin
    ya_ref[...] = (hs_ref[...] * jax.nn.gelu(ga_ref[...])).astype(BF16)


def _rglru(z, conv_prev, h0, conv_w, conv_b, w_r, b_r, w_i, b_i, lam, layer, *, tc):
    b, t, _ = z.shape
    d_rnn = conv_w.shape[-1]
    n_blocks = w_r.shape[1]
    vec = lambda: pl.BlockSpec((None, 1, d_rnn), lambda bi, ti: (layer, 0, 0))
    gate_w = lambda: pl.BlockSpec((None, n_blocks, LANES, LANES), lambda bi, ti: (layer, 0, 0, 0))
    return pl.pallas_call(
        functools.partial(_rglru_kernel, n_blocks=n_blocks),
        out_shape=(jax.ShapeDtypeStruct((b, t, d_rnn), BF16),
                   jax.ShapeDtypeStruct((b, CONV_WIDTH - 1, d_rnn), F32),
                   jax.ShapeDtypeStruct((b, 1, d_rnn), F32)),
        grid=(b, t // tc),
        in_specs=[
            pl.BlockSpec((None, tc, d_rnn), lambda bi, ti: (bi, ti, 0)),
            pl.BlockSpec((None, tc, d_rnn), lambda bi, ti: (bi, ti, 1)),
            pl.BlockSpec((None, CONV_WIDTH - 1, d_rnn), lambda bi, ti: (bi, 0, 0)),
            pl.BlockSpec((None, 1, d_rnn), lambda bi, ti: (bi, 0, 0)),
            pl.BlockSpec((None, CONV_WIDTH, d_rnn), lambda bi, ti: (layer, 0, 0)),
            vec(), gate_w(), vec(), gate_w(), vec(), vec(),
        ],
        out_specs=(pl.BlockSpec((None, tc, d_rnn), lambda bi, ti: (bi, ti, 0)),
                   pl.BlockSpec((None, CONV_WIDTH - 1, d_rnn), lambda bi, ti: (bi, 0, 0)),
                   pl.BlockSpec((None, 1, d_rnn), lambda bi, ti: (bi, 0, 0))),
        scratch_shapes=[pltpu.VMEM((tc + SUBLANES, d_rnn), F32), pltpu.VMEM((tc, d_rnn), F32),
                        pltpu.VMEM((tc, d_rnn), F32), pltpu.VMEM((tc, d_rnn), F32),
                        pltpu.VMEM((1, d_rnn), F32)],
        compiler_params=_params("parallel", "arbitrary"),
        name="rglru",
    )(z, z, conv_prev, h0, conv_w, conv_b, w_r, b_r, w_i, b_i, lam)


def _attn_prompt_kernel(q_ref, k_ref, vt_ref, o_ref, s0_ref, s1_ref, m_ref, acc_ref, *, cw):
    qi = pl.program_id(2)
    tq = q_ref.shape[-1]
    tk = vt_ref.shape[-1]
    hd = o_ref.shape[-1]
    groups = range(tq // cw)

    m_ref[...] = jnp.full(m_ref.shape, NEG_BIG, F32)
    acc_ref[...] = jnp.zeros(acc_ref.shape, F32)

    def qk(jb, s_ref, grps):
        k = k_ref[pl.ds(pl.multiple_of(jb * tk, tk), tk), :]
        for c in grps:
            sl = slice(c * cw, (c + 1) * cw)
            s_ref[:, sl] = jnp.dot(k, q_ref[:, sl], preferred_element_type=F32)

    def softmax_pv(jb, s_ref, grps, k_lo):
        vt = vt_ref[jb]
        for c in grps:
            sl = slice(c * cw, (c + 1) * cw)
            s = s_ref[:, sl]
            if k_lo is not None and k_lo + tk - 1 > c * cw:
                key = lax.broadcasted_iota(jnp.int32, (tk, cw), 0) + (k_lo - c * cw)
                qry = lax.broadcasted_iota(jnp.int32, (tk, cw), 1)
                s = jnp.where(key <= qry, s, NEG_BIG)
            m_prev = m_ref[:, sl]
            m_new = jnp.maximum(m_prev, jnp.max(s, axis=0, keepdims=True))
            alpha = jnp.exp2(m_prev - m_new)
            p = jnp.exp2(s - m_new)
            acc_ref[:, sl] = alpha * acc_ref[:, sl] + jnp.dot(vt, p.astype(BF16),
                                                              preferred_element_type=F32)
            m_ref[:, sl] = m_new

    qk(0, s0_ref, groups)

    def body(i, carry):
        qk(2 * i + 1, s1_ref, groups)
        softmax_pv(2 * i, s0_ref, groups, None)
        qk(2 * i + 2, s0_ref, groups)
        softmax_pv(2 * i + 1, s1_ref, groups, None)
        return carry

    lax.fori_loop(0, qi, body, 0)

    late = [c for c in groups if c * cw + cw - 1 >= tk]
    qk(2 * qi + 1, s1_ref, late)
    softmax_pv(2 * qi, s0_ref, groups, 0)
    softmax_pv(2 * qi + 1, s1_ref, late, tk)

    o_ref[...] = (acc_ref[0:hd, :] / acc_ref[hd:hd + 1, :]).T.astype(o_ref.dtype)


def _attn_prompt(qt, kp, vt, *, tq, cw):
    b, h, s, dq = kp.shape
    _, _, nblk, hdp, tk = vt.shape
    hd = hdp - BF16_ROWS
    assert tq == 2 * tk and tk % cw == 0
    return pl.pallas_call(
        functools.partial(_attn_prompt_kernel, cw=cw),
        out_shape=jax.ShapeDtypeStruct((b, s, h * hd), BF16),
        grid=(b, h, s // tq),
        in_specs=[
            pl.BlockSpec((None, None, dq, tq), lambda bi, hi, qi: (bi, hi, 0, qi)),
            pl.BlockSpec((None, None, s, dq), lambda bi, hi, qi: (bi, hi, 0, 0)),
            pl.BlockSpec((None, None, nblk, hdp, tk), lambda bi, hi, qi: (bi, hi, 0, 0, 0)),
        ],
        out_specs=pl.BlockSpec((None, tq, hd), lambda bi, hi, qi: (bi, qi, hi)),
        scratch_shapes=[pltpu.VMEM((tk, tq), F32), pltpu.VMEM((tk, tq), F32),
                        pltpu.VMEM((1, tq), F32), pltpu.VMEM((hdp, tq), F32)],
        compiler_params=_params("parallel", "parallel", "arbitrary"),
        name="attn_prompt",
    )(qt, kp, vt)


def _cumsum_lanes_kernel(x_ref, o_ref):
    rows, n = x_ref.shape
    r_i = lax.broadcasted_iota(jnp.int32, (LANES, LANES), 0)
    c_i = lax.broadcasted_iota(jnp.int32, (LANES, LANES), 1)
    upper = jnp.where(r_i <= c_i, 1.0, 0.0).astype(BF16)
    carry = jnp.zeros((rows, 1), F32)
    for ch in range(n // LANES):
        sl = slice(ch * LANES, (ch + 1) * LANES)
        cs = carry
        for piece in _split3(x_ref[:, sl]):
            cs = cs + jnp.dot(piece.astype(BF16), upper, preferred_element_type=F32)
        o_ref[:, sl] = cs
        carry = cs[:, LANES - 1:LANES]


def _cumsum_lanes(x, *, tr):
    rows, n = x.shape
    return pl.pallas_call(
        _cumsum_lanes_kernel,
        out_shape=jax.ShapeDtypeStruct((rows, n), F32),
        grid=(rows // tr,),
        in_specs=[pl.BlockSpec((tr, n), lambda i: (i, 0))],
        out_specs=pl.BlockSpec((tr, n), lambda i: (i, 0)),
        compiler_params=_params("parallel"),
        name="cumsum_lanes",
    )(x)


def _attn_sample_kernel(zq_ref, zk_ref, zv_ref, fl_ref, bf_ref, qg_ref, kg_ref,
                        kc_ref, vc_ref, cp_ref, yb_ref, ko_ref, vo_ref, lf_ref, *, scale):
    n_heads, p_len, _ = kc_ref.shape
    t = zq_ref.shape[0]
    lane = lax.broadcasted_iota(jnp.int32, (t, LANES), 1)
    row = lax.broadcasted_iota(jnp.int32, (t, LANES), 0)

    logf_all = _log_sigmoid(fl_ref[...] + bf_ref[...])
    lf_ref[...] = logf_all

    for h in range(n_heads):
        sl = slice(h * LANES, (h + 1) * LANES)
        cl = _cumsum_rows(jnp.broadcast_to(logf_all[:, h:h + 1], (t, LANES)))
        qn = _rms(zq_ref[:, sl], qg_ref[...]) * scale
        kn = _rms(zk_ref[:, sl], kg_ref[...])
        v = zv_ref[:, sl]
        ko_ref[h] = kn
        vo_ref[h] = v

        cp = cp_ref[h]
        s_p = lax.dot_general(qn.astype(BF16), kc_ref[h].astype(BF16), (((1,), (1,)), ((), ())),
                              preferred_element_type=F32)
        s_p = s_p + cl[:, 0:1] + (cp[:, p_len - 1:p_len] - cp)

        q_aug = jnp.concatenate([qn, _aug_q(cl, lane)], axis=1).astype(BF16)
        k_aug = jnp.concatenate([kn, _aug_k(cl, lane)], axis=1).astype(BF16)
        k_pad = jnp.concatenate([k_aug, jnp.zeros((LANES - t, 2 * LANES), BF16)], axis=0)
        v_pad = jnp.concatenate([v.astype(BF16), jnp.zeros((LANES - t, LANES), BF16)], axis=0)
        s_n = lax.dot_general(q_aug, k_pad, (((1,), (1,)), ((), ())), preferred_element_type=F32)
        s_n = jnp.where(lane <= row, s_n, NEG_BIG)

        m = jnp.maximum(jnp.max(s_p, axis=1, keepdims=True), jnp.max(s_n, axis=1, keepdims=True))
        p_p = jnp.exp(s_p - m)
        p_n = jnp.exp(s_n - m)
        l = jnp.sum(p_p, axis=1, keepdims=True) + jnp.sum(p_n, axis=1, keepdims=True)
        o = jnp.dot(p_p.astype(BF16), vc_ref[h].astype(BF16), preferred_element_type=F32)
        o = o + jnp.dot(p_n.astype(BF16), v_pad, preferred_element_type=F32)
        yb_ref[:, sl] = (o / l).astype(yb_ref.dtype)


def _attn_sample(z, b_f, q_gain, k_gain, cache_k, cache_v, c_past, layer, *, n_heads, col0):
    b, t, _ = z.shape
    hd = LANES
    p_len = cache_k.shape[-2]
    scale = float(hd) ** -0.5
    d_attn = n_heads * hd
    sec0 = col0 // n_heads
    zspec = lambda sec: pl.BlockSpec((None, t, d_attn), lambda bi: (bi, 0, sec0 + sec))
    cache = lambda: pl.BlockSpec((None, None, n_heads, p_len, hd), lambda bi: (layer, bi, 0, 0, 0))
    heads = lambda: pl.BlockSpec((None, n_heads, t, hd), lambda bi: (bi, 0, 0, 0))
    return pl.pallas_call(
        functools.partial(_attn_sample_kernel, scale=scale),
        out_shape=(jax.ShapeDtypeStruct((b, t, d_attn), BF16),
                   jax.ShapeDtypeStruct((b, n_heads, t, hd), F32),
                   jax.ShapeDtypeStruct((b, n_heads, t, hd), F32),
                   jax.ShapeDtypeStruct((b, t, LANES), F32)),
        grid=(b,),
        in_specs=[
            zspec(0), zspec(1), zspec(2),
            pl.BlockSpec((None, t, LANES), lambda bi: (bi, 0, col0 + 3 * n_heads)),
            pl.BlockSpec((None, 1, LANES), lambda bi: (layer, 0, 0)),
            pl.BlockSpec((None, 1, hd), lambda bi: (layer, 0, 0)),
            pl.BlockSpec((None, 1, hd), lambda bi: (layer, 0, 0)),
            cache(), cache(),
            pl.BlockSpec((None, None, n_heads, 1, p_len), lambda bi: (layer, bi, 0, 0, 0)),
        ],
        out_specs=(pl.BlockSpec((None, t, d_attn), lambda bi: (bi, 0, 0)),
                   heads(), heads(),
                   pl.BlockSpec((None, t, LANES), lambda bi: (bi, 0, 0))),
        compiler_params=_params("parallel"),
        name="attn_sample",
    )(z, z, z, z, b_f, q_gain, k_gain, cache_k, cache_v, c_past)


def _merge_kernel(ya_ref, yb_ref, ga_ref, gb_ref, wa_ref, wb_ref, o_ref, *, cn):
    ya, yb = ya_ref[...], yb_ref[...]
    for j0 in range(0, o_ref.shape[-1], cn):
        sl = slice(j0, j0 + cn)
        a = jnp.dot(ya, wa_ref[:, sl], preferred_element_type=F32)
        b = jnp.dot(yb, wb_ref[:, sl], preferred_element_type=F32)
        o_ref[:, sl] = (_sigmoid(ga_ref[:, sl]) * a + _sigmoid(gb_ref[:, sl]) * b).astype(o_ref.dtype)


def _resident(shape, index_map):
    return pl.BlockSpec(shape, index_map, pipeline_mode=pl.Buffered(1))


def _merge(ya, yb, z, w_a, w_b, layer, *, gate_col, tm):
    n, d_a = ya.shape
    d_b = yb.shape[-1]
    d = w_a.shape[-1]
    ga0 = gate_col // d
    return pl.pallas_call(
        functools.partial(_merge_kernel, cn=_tile(d, 1024)),
        out_shape=jax.ShapeDtypeStruct((n, d), BF16),
        grid=(n // tm,),
        in_specs=[
            pl.BlockSpec((tm, d_a), lambda i: (i, 0)),
            pl.BlockSpec((tm, d_b), lambda i: (i, 0)),
            pl.BlockSpec((tm, d), lambda i: (i, ga0)),
            pl.BlockSpec((tm, d), lambda i: (i, ga0 + 1)),
            _resident((None, d_a, d), lambda i: (layer, 0, 0)),
            _resident((None, d_b, d), lambda i: (layer, 0, 0)),
        ],
        out_specs=pl.BlockSpec((tm, d), lambda i: (i, 0)),
        compiler_params=_params("parallel"),
        name="merge",
    )(ya, yb, z, z, w_a, w_b)


def _matmul_res_kernel(m_ref, w_ref, x_ref, o_ref):
    o_ref[...] = x_ref[...] + jnp.dot(m_ref[...], w_ref[...], preferred_element_type=F32)


def _matmul_res(m, w, x, layer, *, tm):
    n, k = m.shape
    d = w.shape[-1]
    return pl.pallas_call(
        _matmul_res_kernel,
        out_shape=jax.ShapeDtypeStruct((n, d), F32),
        grid=(n // tm,),
        in_specs=[
            pl.BlockSpec((tm, k), lambda i: (i, 0)),
            _resident((None, k, d), lambda i: (layer, 0, 0)),
            pl.BlockSpec((tm, d), lambda i: (i, 0)),
        ],
        out_specs=pl.BlockSpec((tm, d), lambda i: (i, 0)),
        compiler_params=_params("parallel"),
        name="matmul_res",
    )(m, w, x)


def _mlp_kernel(x_ref, g_ref, wu_ref, wd_ref, o_ref, xn_ref, acc_ref):
    f = pl.program_id(1)

    @pl.when(f == 0)
    def _():
        x = x_ref[...]
        xn_ref[...] = _rms(x, g_ref[...]).astype(BF16)
        acc_ref[...] = x

    hid = jnp.maximum(jnp.dot(xn_ref[...], wu_ref[...], preferred_element_type=F32), 0.0)
    acc_ref[...] += jnp.dot((hid * hid).astype(BF16), wd_ref[...], preferred_element_type=F32)

    @pl.when(f == pl.num_programs(1) - 1)
    def _():
        o_ref[...] = acc_ref[...]


def _mlp(x, g, w_up, w_down, layer, *, tm, tf):
    n, d = x.shape
    d_ff = w_up.shape[-1]
    return pl.pallas_call(
        _mlp_kernel,
        out_shape=jax.ShapeDtypeStruct((n, d), F32),
        grid=(n // tm, d_ff // tf),
        in_specs=[
            pl.BlockSpec((tm, d), lambda i, f: (i, 0)),
            pl.BlockSpec((None, 1, d), lambda i, f: (layer, 0, 0)),
            pl.BlockSpec((None, d, tf), lambda i, f: (layer, 0, f)),
            pl.BlockSpec((None, tf, d), lambda i, f: (layer, f, 0)),
        ],
        out_specs=pl.BlockSpec((tm, d), lambda i, f: (i, 0)),
        scratch_shapes=[pltpu.VMEM((tm, d), BF16), pltpu.VMEM((tm, d), F32)],
        compiler_params=_params("parallel", "arbitrary"),
        name="mlp",
    )(x, g, w_up, w_down)


def _ple_kernel(x_ref, g_ref, wg_ref, p_ref, we_ref, o_ref, *, cn):
    xn = _rms(x_ref[...], g_ref[...]).astype(BF16)
    pe = p_ref[...].astype(BF16)
    for j0 in range(0, o_ref.shape[-1], cn):
        sl = slice(j0, j0 + cn)
        gate = _sigmoid(jnp.dot(xn, wg_ref[:, sl], preferred_element_type=F32))
        emb = jnp.dot(pe, we_ref[:, sl], preferred_element_type=F32)
        o_ref[:, sl] = x_ref[:, sl] + gate * emb


def _ple(x, g, w_pg, p, w_pe, layer, *, tm):
    n, d = x.shape
    d_ple = p.shape[-1]
    return pl.pallas_call(
        functools.partial(_ple_kernel, cn=_tile(d, 1024)),
        out_shape=jax.ShapeDtypeStruct((n, d), F32),
        grid=(n // tm,),
        in_specs=[
            pl.BlockSpec((tm, d), lambda i: (i, 0)),
            pl.BlockSpec((None, 1, d), lambda i: (layer, 0, 0)),
            _resident((None, d, d), lambda i: (layer, 0, 0)),
            pl.BlockSpec((None, tm, d_ple), lambda i: (layer, i, 0)),
            _resident((None, d_ple, d), lambda i: (layer, 0, 0)),
        ],
        out_specs=pl.BlockSpec((tm, d), lambda i: (i, 0)),
        compiler_params=_params("parallel"),
        name="ple",
    )(x, g, w_pg, p, w_pe)


def _tile(n, pref):
    t = min(n, pref)
    while n % t:
        t //= 2
    return t


def kernel(x_prompt, x_sample, cache_k, cache_v, cache_logf, state_conv, state_lru, p_prompt, p_sample, norm_mix, norm_mlp, norm_ple, w_in, b_f, conv_w, conv_b, w_rg, b_rg, w_ig, b_ig, lru_lambda, q_gain, k_gain, w_a_out, w_b_out, w_o, w_up, w_down, w_pe, w_pg):
    depth, d, _ = w_in.shape
    bp, sp, _ = x_prompt.shape
    bs, ts, _ = x_sample.shape
    d_rnn = conv_w.shape[-1]
    n_heads = b_f.shape[-1]
    hd = q_gain.shape[-1]
    d_attn = n_heads * hd
    d_ple = w_pe.shape[1]
    p_len = cache_k.shape[-2]
    assert hd == LANES and w_rg.shape[-1] == LANES and n_heads <= LANES

    o_q = 2 * d_rnn
    o_fl = o_q + 3 * d_attn
    o_gate = o_fl + n_heads
    fl_pad = _tile(d_attn, 1024)
    w_rnn = w_in[:, :, :o_q].astype(BF16)
    w_gates = w_in[:, :, o_gate:].astype(BF16)
    w_qkv = w_in[:, :, o_q:o_fl].astype(BF16)
    w_fl = jnp.pad(w_in[:, :, o_fl:o_gate], ((0, 0), (0, 0), (0, fl_pad - n_heads))).astype(BF16)
    gate_col = o_q
    n_ag = o_q + 2 * d
    qkv_tile0 = n_ag // LANES

    row = lambda a: a.reshape(depth, 1, a.shape[-1])
    bf_pad = jnp.pad(b_f, ((0, 0), (0, LANES - n_heads))).reshape(depth, 1, LANES)
    g_mix, g_mlp, g_ple = row(norm_mix), row(norm_mlp), row(norm_ple)
    qg, kg = row(q_gain), row(k_gain)
    cb, br, bi, lam = row(conv_b), row(b_rg), row(b_ig), row(lru_lambda)
    w_r16, w_i16 = w_rg.astype(BF16), w_ig.astype(BF16)
    w_a16, w_b16, w_o16 = w_a_out.astype(BF16), w_b_out.astype(BF16), w_o.astype(BF16)
    w_up16, w_dn16 = w_up.astype(BF16), w_down.astype(BF16)
    w_pe16, w_pg16 = w_pe.astype(BF16), w_pg.astype(BF16)

    n_p, n_s = bp * sp, bs * ts
    pp = p_prompt.reshape(depth, n_p, d_ple)
    ps = p_sample.reshape(depth, n_s, d_ple)

    rows = depth * bs * n_heads
    c_past = _cumsum_lanes(cache_logf.reshape(rows, p_len), tr=_tile(rows, 256))
    c_past = c_past.reshape(depth, bs, n_heads, 1, p_len)

    tn = _tile(d, 1024)
    tf = _tile(w_up.shape[-1], 1024)

    def tokenwise(x, y_a, y_b, z, pe, layer):
        tm = _tile(x.shape[0], 512)
        merged = _merge(y_a, y_b, z, w_a16, w_b16, layer, gate_col=gate_col, tm=tm)
        x = _matmul_res(merged, w_o16, x, layer, tm=tm)
        x = _mlp(x, g_mlp, w_up16, w_dn16, layer, tm=tm, tf=tf)
        return _ple(x, g_ple, w_pg16, pe, w_pe16, layer, tm=tm)

    xp = x_prompt.reshape(n_p, d)
    xs = x_sample.reshape(n_s, d)
    tm_p = _tile(n_p, 1024)
    tm_s = _tile(n_s, 1024)
    tq = _tile(sp, 512)
    conv0 = jnp.zeros((bp, CONV_WIDTH - 1, d_rnn), F32)
    h00 = jnp.zeros((bp, 1, d_rnn), F32)
    outs = [[] for _ in range(10)]
    for layer in range(depth):
        z = _normed_matmul(xp, g_mix, [w_rnn, w_gates], layer, tm=tm_p, tn=tn)
        qp, kp, vt, k1, v1, lf1 = _qkv_prompt(xp.reshape(bp, sp, d), g_mix, w_qkv, w_fl, bf_pad, qg, kg,
                                              layer, n_heads=n_heads, tm=tq)
        y_a, c1, l1 = _rglru(z.reshape(bp, sp, -1), conv0, h00, conv_w, cb, w_r16, br, w_i16, bi, lam,
                             layer, tc=_tile(sp, 256))
        y_b = _attn_prompt(qp, kp, vt, tq=_tile(sp, 2 * tq), cw=_tile(tq, 256))
        xp = tokenwise(xp, y_a.reshape(n_p, d_rnn), y_b.reshape(n_p, d_attn), z, pp, layer)
        f1 = lf1[:, :, :n_heads].transpose(0, 2, 1)

        zs = _normed_matmul(xs, g_mix, [w_rnn, w_gates, w_qkv, w_fl], layer, tm=tm_s, tn=tn)
        zs3 = zs.reshape(bs, ts, -1)
        y_a, c2, l2 = _rglru(zs3, state_conv[layer], state_lru[layer].reshape(bs, 1, d_rnn), conv_w, cb,
                             w_r16, br, w_i16, bi, lam, layer, tc=ts)
        y_b, k2, v2, lf2 = _attn_sample(zs3, bf_pad, qg, kg, cache_k, cache_v, c_past, layer,
                                        n_heads=n_heads, col0=qkv_tile0)
        xs = tokenwise(xs, y_a.reshape(n_s, d_rnn), y_b.reshape(n_s, d_attn), zs, ps, layer)
        f2 = lf2[:, :, :n_heads].transpose(0, 2, 1)

        for lst, val in zip(outs, (k1, v1, f1, c1, l1.reshape(bp, d_rnn),
                                   k2, v2, f2, c2, l2.reshape(bs, d_rnn))):
            lst.append(val)

    return (xp.reshape(bp, sp, d), xs.reshape(bs, ts, d)) + tuple(jnp.stack(o) for o in outs)
```

```python
import functools

import jax
import jax.numpy as jnp
from jax import lax
from jax.experimental import pallas as pl
from jax.experimental.pallas import tpu as pltpu

F32 = jnp.float32
BF16 = jnp.bfloat16

EPS = 1e-6
LRU_C = 8.0
CONV_WIDTH = 4
LANES = 128
SUBLANES = 8
BF16_ROWS = 16
AUG = 128
NEG_BIG = -1e30
LOG2_E = 1.4426950408889634
VMEM_LIMIT = 56 * 1024 * 1024


def _params(*sem):
    return pltpu.CompilerParams(dimension_semantics=sem, vmem_limit_bytes=VMEM_LIMIT)


def _rms(x, g):
    return x * lax.rsqrt(jnp.mean(x * x, axis=-1, keepdims=True) + EPS) * g


def _log_sigmoid(x):
    return jnp.minimum(x, 0.0) - jnp.log1p(jnp.exp(-jnp.abs(x)))


def _sigmoid(x):
    return 1.0 / (1.0 + jnp.exp(-x))


def _split3(c):
    hi = c.astype(BF16).astype(F32)
    r = c - hi
    mid = r.astype(BF16).astype(F32)
    lo = (r - mid).astype(BF16).astype(F32)
    return hi, mid, lo


def _aug_q(c, lane):
    hi, mid, lo = _split3(c)
    return jnp.where(lane == 0, hi, jnp.where(lane == 1, mid, jnp.where(
        lane == 2, lo, jnp.where(lane < 6, 1.0, 0.0))))


def _aug_k(c, lane):
    hi, mid, lo = _split3(c)
    return jnp.where(lane < 3, 1.0, jnp.where(lane == 3, -hi, jnp.where(
        lane == 4, -mid, jnp.where(lane == 5, -lo, 0.0))))


def _cumsum_rows(x):
    n = x.shape[0]
    row = lax.broadcasted_iota(jnp.int32, x.shape, 0)
    d = 1
    while d < n:
        x = x + jnp.where(row >= d, pltpu.roll(x, d, 0), 0.0)
        d *= 2
    return x


def _normed_matmul_kernel(x_ref, g_ref, w_ref, o_ref, xn_ref):
    @pl.when(pl.program_id(1) == 0)
    def _():
        xn_ref[...] = _rms(x_ref[...], g_ref[...]).astype(BF16)

    o_ref[...] = jnp.dot(xn_ref[...], w_ref[...], preferred_element_type=F32)


def _normed_matmul(x, g, w, layer, *, n_out, tm, tn):
    n, d = x.shape
    return pl.pallas_call(
        _normed_matmul_kernel,
        out_shape=jax.ShapeDtypeStruct((n, n_out), F32),
        grid=(n // tm, n_out // tn),
        in_specs=[
            pl.BlockSpec((tm, d), lambda i, j: (i, 0)),
            pl.BlockSpec((None, 1, d), lambda i, j: (layer, 0, 0)),
            pl.BlockSpec((None, d, tn), lambda i, j: (layer, 0, j)),
        ],
        out_specs=pl.BlockSpec((tm, tn), lambda i, j: (i, j)),
        scratch_shapes=[pltpu.VMEM((tm, d), BF16)],
        compiler_params=_params("parallel", "arbitrary"),
        name="normed_matmul",
    )(x, g, w)


def _qkv_prompt_kernel(x_ref, g_ref, wq_ref, wk_ref, wv_ref, wfl_ref, bf_ref, qg_ref, kg_ref,
                       qt_ref, kp_ref, vt_ref, ko_ref, vo_ref, lf_ref, carry_ref, *, n_heads, scale):
    tm = x_ref.shape[0]
    xn = _rms(x_ref[...], g_ref[...]).astype(BF16)
    logf = _log_sigmoid(jnp.dot(xn, wfl_ref[...], preferred_element_type=F32) + bf_ref[...])
    lf_ref[...] = logf

    @pl.when(pl.program_id(1) == 0)
    def _():
        carry_ref[...] = jnp.zeros_like(carry_ref)

    c = _cumsum_rows(logf) + carry_ref[...]
    carry_ref[...] = c[tm - 1:tm, :]
    hi, mid, lo = _split3(c * LOG2_E)
    lane = lax.broadcasted_iota(jnp.int32, (tm, LANES), 1)

    def to_lane(x, dst, h):
        return pltpu.roll(x, (dst - h) % LANES, 1)

    hi_t, mid_t, lo_t = hi.T, mid.T, lo.T
    row = lax.broadcasted_iota(jnp.int32, (BF16_ROWS, tm), 0)
    qg = qg_ref[...] * (scale * LOG2_E)
    zq = jnp.dot(xn, wq_ref[...], preferred_element_type=F32)
    for h in range(n_heads):
        aug = jnp.where(row == 0, hi_t[h:h + 1], jnp.where(row == 1, mid_t[h:h + 1], jnp.where(
            row == 2, lo_t[h:h + 1], jnp.where(row < 6, 1.0, 0.0))))
        qt_ref[h, 0:LANES, :] = _rms(zq[:, h * LANES:(h + 1) * LANES], qg).T.astype(BF16)
        qt_ref[h, LANES:LANES + BF16_ROWS, :] = aug.astype(BF16)
        qt_ref[h, LANES + BF16_ROWS:LANES + AUG, :] = jnp.zeros((AUG - BF16_ROWS, tm), BF16)

    zk = jnp.dot(xn, wk_ref[...], preferred_element_type=F32)
    for h in range(n_heads):
        aug = jnp.where(lane < 3, 1.0, jnp.where(lane == 3, -to_lane(hi, 3, h), jnp.where(
            lane == 4, -to_lane(mid, 4, h), jnp.where(lane == 5, -to_lane(lo, 5, h), 0.0))))
        kn = _rms(zk[:, h * LANES:(h + 1) * LANES], kg_ref[...])
        ko_ref[h] = kn
        kp_ref[h, :, 0:LANES] = kn.astype(BF16)
        kp_ref[h, :, LANES:LANES + AUG] = aug.astype(BF16)

    ones_row = jnp.where(row == 0, 1.0, 0.0).astype(BF16)
    zv = jnp.dot(xn, wv_ref[...], preferred_element_type=F32)
    for h in range(n_heads):
        zh = zv[:, h * LANES:(h + 1) * LANES]
        vo_ref[h] = zh
        vt_ref[h, 0:LANES, :] = zh.T.astype(BF16)
        vt_ref[h, LANES:LANES + BF16_ROWS, :] = ones_row


def _qkv_prompt(x, g, w, b_f, q_gain, k_gain, layer, *, n_heads, col0, tm):
    b, s, d = x.shape
    sec0 = col0 // (n_heads * LANES)
    fl_tile = col0 // LANES + 3 * n_heads
    hd = LANES
    d_attn = n_heads * hd
    scale = float(hd) ** -0.5
    head_major = lambda width, dt: jax.ShapeDtypeStruct((b, n_heads, s, width), dt)
    head_spec = lambda width: pl.BlockSpec((None, n_heads, tm, width), lambda bi, i: (bi, 0, i, 0))
    resident = lambda shape, idx: pl.BlockSpec(shape, idx, pipeline_mode=pl.Buffered(1))
    wsec = lambda sec: resident((None, d, d_attn), lambda bi, i: (layer, 0, sec0 + sec))
    return pl.pallas_call(
        functools.partial(_qkv_prompt_kernel, n_heads=n_heads, scale=scale),
        out_shape=(jax.ShapeDtypeStruct((b, n_heads, hd + AUG, s), BF16), head_major(hd + AUG, BF16),
                   jax.ShapeDtypeStruct((b, n_heads, s // tm, hd + BF16_ROWS, tm), BF16),
                   head_major(hd, F32), head_major(hd, F32),
                   jax.ShapeDtypeStruct((b, s, LANES), F32)),
        grid=(b, s // tm),
        in_specs=[
            pl.BlockSpec((None, tm, d), lambda bi, i: (bi, i, 0)),
            pl.BlockSpec((None, 1, d), lambda bi, i: (layer, 0, 0)),
            wsec(0), wsec(1), wsec(2),
            resident((None, d, LANES), lambda bi, i: (layer, 0, fl_tile)),
            pl.BlockSpec((None, 1, LANES), lambda bi, i: (layer, 0, 0)),
            pl.BlockSpec((None, 1, hd), lambda bi, i: (layer, 0, 0)),
            pl.BlockSpec((None, 1, hd), lambda bi, i: (layer, 0, 0)),
        ],
        out_specs=(pl.BlockSpec((None, n_heads, hd + AUG, tm), lambda bi, i: (bi, 0, 0, i)),
                   head_spec(hd + AUG),
                   pl.BlockSpec((None, n_heads, None, hd + BF16_ROWS, tm), lambda bi, i: (bi, 0, i, 0, 0)),
                   head_spec(hd), head_spec(hd),
                   pl.BlockSpec((None, tm, LANES), lambda bi, i: (bi, i, 0))),
        scratch_shapes=[pltpu.VMEM((1, LANES), F32)],
        compiler_params=_params("parallel", "arbitrary"),
        name="qkv_prompt",
    )(x, g, w, w, w, w, b_f, q_gain, k_gain)


def _rglru_kernel(xa_ref, ga_ref, cprev_ref, h0_ref, cw_ref, cb_ref, wr_ref, br_ref,
                  wi_ref, bi_ref, lam_ref, ya_ref, cnew_ref, hlast_ref,
                  xpad_ref, a_ref, u_ref, hs_ref, hc_ref, *, n_blocks):
    t = pl.program_id(1)
    tc = xa_ref.shape[0]
    pad = SUBLANES

    @pl.when(t == 0)
    def _():
        xpad_ref[pad - (CONV_WIDTH - 1):pad, :] = cprev_ref[...]
        hc_ref[...] = h0_ref[...]

    xa = xa_ref[...]
    xpad_ref[pad:pad + tc, :] = xa
    xc = cb_ref[...] + cw_ref[CONV_WIDTH - 1:CONV_WIDTH, :] * xa
    for jj in range(CONV_WIDTH - 1):
        off = pad - (CONV_WIDTH - 1) + jj
        xc = xc + cw_ref[jj:jj + 1, :] * xpad_ref[off:off + tc, :]
    tail = xa[tc - (CONV_WIDTH - 1):tc, :]
    xpad_ref[pad - (CONV_WIDTH - 1):pad, :] = tail
    cnew_ref[...] = tail

    lam = lam_ref[...]
    softplus_neg_lam = jnp.maximum(-lam, 0.0) + jnp.log1p(jnp.exp(-jnp.abs(lam)))
    for nb in range(n_blocks):
        sl = slice(nb * LANES, (nb + 1) * LANES)
        xb = xc[:, sl]
        xb16 = xb.astype(BF16)
        r = _sigmoid(jnp.dot(xb16, wr_ref[nb], preferred_element_type=F32) + br_ref[:, sl])
        ig = _sigmoid(jnp.dot(xb16, wi_ref[nb], preferred_element_type=F32) + bi_ref[:, sl])
        log_a = (-LRU_C) * r * softplus_neg_lam[:, sl]
        a = jnp.exp(log_a)
        a_ref[:, sl] = a
        u_ref[:, sl] = jnp.sqrt(jnp.tanh(-log_a) * (1.0 + a * a)) * (ig * xb)

    def step(g, h):
        base = pl.multiple_of(g * SUBLANES, SUBLANES)
        for r_ in range(SUBLANES):
            h = a_ref[pl.ds(base + r_, 1), :] * h + u_ref[pl.ds(base + r_, 1), :]
            hs_ref[pl.ds(base + r_, 1), :] = h
        return h

    h_fin = lax.fori_loop(0, tc // SUBLANES, step, hc_ref[...])
    hc_ref[...] = h_fin
    hlast_ref[...] = h_fin
    ya_ref[...] = (hs_ref[...] * jax.nn.gelu(ga_ref[...])).astype(BF16)


def _rglru(z, conv_prev, h0, conv_w, conv_b, w_r, b_r, w_i, b_i, lam, layer, *, tc):
    b, t, _ = z.shape
    d_rnn = conv_w.shape[-1]
    n_blocks = w_r.shape[1]
    vec = lambda: pl.BlockSpec((None, 1, d_rnn), lambda bi, ti: (layer, 0, 0))
    gate_w = lambda: pl.BlockSpec((None, n_blocks, LANES, LANES), lambda bi, ti: (layer, 0, 0, 0))
    return pl.pallas_call(
        functools.partial(_rglru_kernel, n_blocks=n_blocks),
        out_shape=(jax.ShapeDtypeStruct((b, t, d_rnn), BF16),
                   jax.ShapeDtypeStruct((b, CONV_WIDTH - 1, d_rnn), F32),
                   jax.ShapeDtypeStruct((b, 1, d_rnn), F32)),
        grid=(b, t // tc),
        in_specs=[
            pl.BlockSpec((None, tc, d_rnn), lambda bi, ti: (bi, ti, 0)),
            pl.BlockSpec((None, tc, d_rnn), lambda bi, ti: (bi, ti, 1)),
            pl.BlockSpec((None, CONV_WIDTH - 1, d_rnn), lambda bi, ti: (bi, 0, 0)),
            pl.BlockSpec((None, 1, d_rnn), lambda bi, ti: (bi, 0, 0)),
            pl.BlockSpec((None, CONV_WIDTH, d_rnn), lambda bi, ti: (layer, 0, 0)),
            vec(), gate_w(), vec(), gate_w(), vec(), vec(),
        ],
        out_specs=(pl.BlockSpec((None, tc, d_rnn), lambda bi, ti: (bi, ti, 0)),
                   pl.BlockSpec((None, CONV_WIDTH - 1, d_rnn), lambda bi, ti: (bi, 0, 0)),
                   pl.BlockSpec((None, 1, d_rnn), lambda bi, ti: (bi, 0, 0))),
        scratch_shapes=[pltpu.VMEM((tc + SUBLANES, d_rnn), F32), pltpu.VMEM((tc, d_rnn), F32),
                        pltpu.VMEM((tc, d_rnn), F32), pltpu.VMEM((tc, d_rnn), F32),
                        pltpu.VMEM((1, d_rnn), F32)],
        compiler_params=_params("parallel", "arbitrary"),
        name="rglru",
    )(z, z, conv_prev, h0, conv_w, conv_b, w_r, b_r, w_i, b_i, lam)


def _attn_prompt_kernel(q_ref, k_ref, vt_ref, o_ref, s0_ref, s1_ref, m_ref, acc_ref, *, cw):
    qi = pl.program_id(2)
    tq = q_ref.shape[-1]
    tk = vt_ref.shape[-1]
    hd = o_ref.shape[-1]
    groups = range(tq // cw)

    m_ref[...] = jnp.full(m_ref.shape, NEG_BIG, F32)
    acc_ref[...] = jnp.zeros(acc_ref.shape, F32)

    def qk(jb, s_ref, grps):
        k = k_ref[pl.ds(pl.multiple_of(jb * tk, tk), tk), :]
        for c in grps:
            sl = slice(c * cw, (c + 1) * cw)
            s_ref[:, sl] = jnp.dot(k, q_ref[:, sl], preferred_element_type=F32)

    def softmax_pv(jb, s_ref, grps, k_lo):
        vt = vt_ref[jb]
        for c in grps:
            sl = slice(c * cw, (c + 1) * cw)
            s = s_ref[:, sl]
            if k_lo is not None and k_lo + tk - 1 > c * cw:
                key = lax.broadcasted_iota(jnp.int32, (tk, cw), 0) + (k_lo - c * cw)
                qry = lax.broadcasted_iota(jnp.int32, (tk, cw), 1)
                s = jnp.where(key <= qry, s, NEG_BIG)
            m_prev = m_ref[:, sl]
            m_new = jnp.maximum(m_prev, jnp.max(s, axis=0, keepdims=True))
            alpha = jnp.exp2(m_prev - m_new)
            p = jnp.exp2(s - m_new)
            acc_ref[:, sl] = alpha * acc_ref[:, sl] + jnp.dot(vt, p.astype(BF16),
                                                              preferred_element_type=F32)
            m_ref[:, sl] = m_new

    qk(0, s0_ref, groups)

    def body(i, carry):
        qk(2 * i + 1, s1_ref, groups)
        softmax_pv(2 * i, s0_ref, groups, None)
        qk(2 * i + 2, s0_ref, groups)
        softmax_pv(2 * i + 1, s1_ref, groups, None)
        return carry

    lax.fori_loop(0, qi, body, 0)

    late = [c for c in groups if c * cw + cw - 1 >= tk]
    qk(2 * qi + 1, s1_ref, late)
    softmax_pv(2 * qi, s0_ref, groups, 0)
    softmax_pv(2 * qi + 1, s1_ref, late, tk)

    o_ref[...] = (acc_ref[0:hd, :] / acc_ref[hd:hd + 1, :]).T.astype(o_ref.dtype)


def _attn_prompt(qt, kp, vt, *, tq, cw):
    b, h, s, dq = kp.shape
    _, _, nblk, hdp, tk = vt.shape
    hd = hdp - BF16_ROWS
    assert tq == 2 * tk and tk % cw == 0
    return pl.pallas_call(
        functools.partial(_attn_prompt_kernel, cw=cw),
        out_shape=jax.ShapeDtypeStruct((b, s, h * hd), BF16),
        grid=(b, h, s // tq),
        in_specs=[
            pl.BlockSpec((None, None, dq, tq), lambda bi, hi, qi: (bi, hi, 0, qi)),
            pl.BlockSpec((None, None, s, dq), lambda bi, hi, qi: (bi, hi, 0, 0)),
            pl.BlockSpec((None, None, nblk, hdp, tk), lambda bi, hi, qi: (bi, hi, 0, 0, 0)),
        ],
        out_specs=pl.BlockSpec((None, tq, hd), lambda bi, hi, qi: (bi, qi, hi)),
        scratch_shapes=[pltpu.VMEM((tk, tq), F32), pltpu.VMEM((tk, tq), F32),
                        pltpu.VMEM((1, tq), F32), pltpu.VMEM((hdp, tq), F32)],
        compiler_params=_params("parallel", "parallel", "arbitrary"),
        name="attn_prompt",
    )(qt, kp, vt)


def _cumsum_lanes_kernel(x_ref, o_ref):
    rows, n = x_ref.shape
    r_i = lax.broadcasted_iota(jnp.int32, (LANES, LANES), 0)
    c_i = lax.broadcasted_iota(jnp.int32, (LANES, LANES), 1)
    upper = jnp.where(r_i <= c_i, 1.0, 0.0).astype(BF16)
    carry = jnp.zeros((rows, 1), F32)
    for ch in range(n // LANES):
        sl = slice(ch * LANES, (ch + 1) * LANES)
        cs = carry
        for piece in _split3(x_ref[:, sl]):
            cs = cs + jnp.dot(piece.astype(BF16), upper, preferred_element_type=F32)
        o_ref[:, sl] = cs
        carry = cs[:, LANES - 1:LANES]


def _cumsum_lanes(x, *, tr):
    rows, n = x.shape
    return pl.pallas_call(
        _cumsum_lanes_kernel,
        out_shape=jax.ShapeDtypeStruct((rows, n), F32),
        grid=(rows // tr,),
        in_specs=[pl.BlockSpec((tr, n), lambda i: (i, 0))],
        out_specs=pl.BlockSpec((tr, n), lambda i: (i, 0)),
        compiler_params=_params("parallel"),
        name="cumsum_lanes",
    )(x)


def _attn_sample_kernel(zq_ref, zk_ref, zv_ref, fl_ref, bf_ref, qg_ref, kg_ref,
                        kc_ref, vc_ref, cp_ref, yb_ref, ko_ref, vo_ref, lf_ref, *, scale):
    n_heads, p_len, _ = kc_ref.shape
    t = zq_ref.shape[0]
    lane = lax.broadcasted_iota(jnp.int32, (t, LANES), 1)
    row = lax.broadcasted_iota(jnp.int32, (t, LANES), 0)

    logf_all = _log_sigmoid(fl_ref[...] + bf_ref[...])
    lf_ref[...] = logf_all

    for h in range(n_heads):
        sl = slice(h * LANES, (h + 1) * LANES)
        cl = _cumsum_rows(jnp.broadcast_to(logf_all[:, h:h + 1], (t, LANES)))
        qn = _rms(zq_ref[:, sl], qg_ref[...]) * scale
        kn = _rms(zk_ref[:, sl], kg_ref[...])
        v = zv_ref[:, sl]
        ko_ref[h] = kn
        vo_ref[h] = v

        cp = cp_ref[h]
        s_p = lax.dot_general(qn.astype(BF16), kc_ref[h].astype(BF16), (((1,), (1,)), ((), ())),
                              preferred_element_type=F32)
        s_p = s_p + cl[:, 0:1] + (cp[:, p_len - 1:p_len] - cp)

        q_aug = jnp.concatenate([qn, _aug_q(cl, lane)], axis=1).astype(BF16)
        k_aug = jnp.concatenate([kn, _aug_k(cl, lane)], axis=1).astype(BF16)
        k_pad = jnp.concatenate([k_aug, jnp.zeros((LANES - t, 2 * LANES), BF16)], axis=0)
        v_pad = jnp.concatenate([v.astype(BF16), jnp.zeros((LANES - t, LANES), BF16)], axis=0)
        s_n = lax.dot_general(q_aug, k_pad, (((1,), (1,)), ((), ())), preferred_element_type=F32)
        s_n = jnp.where(lane <= row, s_n, NEG_BIG)

        m = jnp.maximum(jnp.max(s_p, axis=1, keepdims=True), jnp.max(s_n, axis=1, keepdims=True))
        p_p = jnp.exp(s_p - m)
        p_n = jnp.exp(s_n - m)
        l = jnp.sum(p_p, axis=1, keepdims=True) + jnp.sum(p_n, axis=1, keepdims=True)
        o = jnp.dot(p_p.astype(BF16), vc_ref[h].astype(BF16), preferred_element_type=F32)
        o = o + jnp.dot(p_n.astype(BF16), v_pad, preferred_element_type=F32)
        yb_ref[:, sl] = (o / l).astype(yb_ref.dtype)


def _attn_sample(z, b_f, q_gain, k_gain, cache_k, cache_v, c_past, layer, *, n_heads, col0):
    b, t, _ = z.shape
    hd = LANES
    p_len = cache_k.shape[-2]
    scale = float(hd) ** -0.5
    d_attn = n_heads * hd
    sec0 = col0 // n_heads
    zspec = lambda sec: pl.BlockSpec((None, t, d_attn), lambda bi: (bi, 0, sec0 + sec))
    cache = lambda: pl.BlockSpec((None, None, n_heads, p_len, hd), lambda bi: (layer, bi, 0, 0, 0))
    heads = lambda: pl.BlockSpec((None, n_heads, t, hd), lambda bi: (bi, 0, 0, 0))
    return pl.pallas_call(
        functools.partial(_attn_sample_kernel, scale=scale),
        out_shape=(jax.ShapeDtypeStruct((b, t, d_attn), BF16),
                   jax.ShapeDtypeStruct((b, n_heads, t, hd), F32),
                   jax.ShapeDtypeStruct((b, n_heads, t, hd), F32),
                   jax.ShapeDtypeStruct((b, t, LANES), F32)),
        grid=(b,),
        in_specs=[
            zspec(0), zspec(1), zspec(2),
            pl.BlockSpec((None, t, LANES), lambda bi: (bi, 0, col0 + 3 * n_heads)),
            pl.BlockSpec((None, 1, LANES), lambda bi: (layer, 0, 0)),
            pl.BlockSpec((None, 1, hd), lambda bi: (layer, 0, 0)),
            pl.BlockSpec((None, 1, hd), lambda bi: (layer, 0, 0)),
            cache(), cache(),
            pl.BlockSpec((None, None, n_heads, 1, p_len), lambda bi: (layer, bi, 0, 0, 0)),
        ],
        out_specs=(pl.BlockSpec((None, t, d_attn), lambda bi: (bi, 0, 0)),
                   heads(), heads(),
                   pl.BlockSpec((None, t, LANES), lambda bi: (bi, 0, 0))),
        compiler_params=_params("parallel"),
        name="attn_sample",
    )(z, z, z, z, b_f, q_gain, k_gain, cache_k, cache_v, c_past)


def _merge_kernel(ya_ref, yb_ref, ga_ref, gb_ref, wa_ref, wb_ref, o_ref, *, cn):
    ya, yb = ya_ref[...], yb_ref[...]
    for j0 in range(0, o_ref.shape[-1], cn):
        sl = slice(j0, j0 + cn)
        a = jnp.dot(ya, wa_ref[:, sl], preferred_element_type=F32)
        b = jnp.dot(yb, wb_ref[:, sl], preferred_element_type=F32)
        o_ref[:, sl] = (_sigmoid(ga_ref[:, sl]) * a + _sigmoid(gb_ref[:, sl]) * b).astype(o_ref.dtype)


def _resident(shape, index_map):
    return pl.BlockSpec(shape, index_map, pipeline_mode=pl.Buffered(1))


def _merge(ya, yb, z, w_a, w_b, layer, *, gate_col, tm):
    n, d_a = ya.shape
    d_b = yb.shape[-1]
    d = w_a.shape[-1]
    ga0 = gate_col // d
    return pl.pallas_call(
        functools.partial(_merge_kernel, cn=_tile(d, 1024)),
        out_shape=jax.ShapeDtypeStruct((n, d), BF16),
        grid=(n // tm,),
        in_specs=[
            pl.BlockSpec((tm, d_a), lambda i: (i, 0)),
            pl.BlockSpec((tm, d_b), lambda i: (i, 0)),
            pl.BlockSpec((tm, d), lambda i: (i, ga0)),
            pl.BlockSpec((tm, d), lambda i: (i, ga0 + 1)),
            _resident((None, d_a, d), lambda i: (layer, 0, 0)),
            _resident((None, d_b, d), lambda i: (layer, 0, 0)),
        ],
        out_specs=pl.BlockSpec((tm, d), lambda i: (i, 0)),
        compiler_params=_params("parallel"),
        name="merge",
    )(ya, yb, z, z, w_a, w_b)


def _matmul_res_kernel(m_ref, w_ref, x_ref, o_ref):
    o_ref[...] = x_ref[...] + jnp.dot(m_ref[...], w_ref[...], preferred_element_type=F32)


def _matmul_res(m, w, x, layer, *, tm):
    n, k = m.shape
    d = w.shape[-1]
    return pl.pallas_call(
        _matmul_res_kernel,
        out_shape=jax.ShapeDtypeStruct((n, d), F32),
        grid=(n // tm,),
        in_specs=[
            pl.BlockSpec((tm, k), lambda i: (i, 0)),
            _resident((None, k, d), lambda i: (layer, 0, 0)),
            pl.BlockSpec((tm, d), lambda i: (i, 0)),
        ],
        out_specs=pl.BlockSpec((tm, d), lambda i: (i, 0)),
        compiler_params=_params("parallel"),
        name="matmul_res",
    )(m, w, x)


def _mlp_kernel(x_ref, g_ref, wu_ref, wd_ref, o_ref, xn_ref):
    @pl.when(pl.program_id(1) == 0)
    def _():
        x = x_ref[...]
        xn_ref[...] = _rms(x, g_ref[...]).astype(BF16)
        o_ref[...] = x

    hid = jnp.maximum(jnp.dot(xn_ref[...], wu_ref[...], preferred_element_type=F32), 0.0)
    o_ref[...] += jnp.dot((hid * hid).astype(BF16), wd_ref[...], preferred_element_type=F32)


def _mlp(x, g, w_up, w_down, layer, *, tm, tf):
    n, d = x.shape
    d_ff = w_up.shape[-1]
    return pl.pallas_call(
        _mlp_kernel,
        out_shape=jax.ShapeDtypeStruct((n, d), F32),
        grid=(n // tm, d_ff // tf),
        in_specs=[
            pl.BlockSpec((tm, d), lambda i, f: (i, 0)),
            pl.BlockSpec((None, 1, d), lambda i, f: (layer, 0, 0)),
            pl.BlockSpec((None, d, tf), lambda i, f: (layer, 0, f)),
            pl.BlockSpec((None, tf, d), lambda i, f: (layer, f, 0)),
        ],
        out_specs=pl.BlockSpec((tm, d), lambda i, f: (i, 0)),
        scratch_shapes=[pltpu.VMEM((tm, d), BF16)],
        compiler_params=_params("parallel", "arbitrary"),
        name="mlp",
    )(x, g, w_up, w_down)


def _ple_kernel(x_ref, g_ref, wg_ref, p_ref, we_ref, o_ref, *, cn):
    xn = _rms(x_ref[...], g_ref[...]).astype(BF16)
    pe = p_ref[...].astype(BF16)
    for j0 in range(0, o_ref.shape[-1], cn):
        sl = slice(j0, j0 + cn)
        gate = _sigmoid(jnp.dot(xn, wg_ref[:, sl], preferred_element_type=F32))
        emb = jnp.dot(pe, we_ref[:, sl], preferred_element_type=F32)
        o_ref[:, sl] = x_ref[:, sl] + gate * emb


def _ple(x, g, w_pg, p, w_pe, layer, *, tm):
    n, d = x.shape
    d_ple = p.shape[-1]
    return pl.pallas_call(
        functools.partial(_ple_kernel, cn=_tile(d, 1024)),
        out_shape=jax.ShapeDtypeStruct((n, d), F32),
        grid=(n // tm,),
        in_specs=[
            pl.BlockSpec((tm, d), lambda i: (i, 0)),
            pl.BlockSpec((None, 1, d), lambda i: (layer, 0, 0)),
            _resident((None, d, d), lambda i: (layer, 0, 0)),
            pl.BlockSpec((None, tm, d_ple), lambda i: (layer, i, 0)),
            _resident((None, d_ple, d), lambda i: (layer, 0, 0)),
        ],
        out_specs=pl.BlockSpec((tm, d), lambda i: (i, 0)),
        compiler_params=_params("parallel"),
        name="ple",
    )(x, g, w_pg, p, w_pe)


def _tile(n, pref):
    t = min(n, pref)
    while n % t:
        t //= 2
    return t


def kernel(x_prompt, x_sample, cache_k, cache_v, cache_logf, state_conv, state_lru, p_prompt, p_sample, norm_mix, norm_mlp, norm_ple, w_in, b_f, conv_w, conv_b, w_rg, b_rg, w_ig, b_ig, lru_lambda, q_gain, k_gain, w_a_out, w_b_out, w_o, w_up, w_down, w_pe, w_pg):
    depth, d, _ = w_in.shape
    bp, sp, _ = x_prompt.shape
    bs, ts, _ = x_sample.shape
    d_rnn = conv_w.shape[-1]
    n_heads = b_f.shape[-1]
    hd = q_gain.shape[-1]
    d_attn = n_heads * hd
    d_ple = w_pe.shape[1]
    p_len = cache_k.shape[-2]
    assert hd == LANES and w_rg.shape[-1] == LANES and n_heads <= LANES

    o_q = 2 * d_rnn
    o_fl = o_q + 3 * d_attn
    o_gate = o_fl + n_heads
    fl_pad = _tile(d_attn, 1024)
    w_fl = lax.optimization_barrier(w_in[:, :, o_fl:o_gate])
    w_fl = jnp.pad(w_fl, ((0, 0), (0, 0), (0, fl_pad - n_heads)))
    w_all = jnp.concatenate([w_in[:, :, :o_q], w_in[:, :, o_gate:], w_in[:, :, o_q:o_fl], w_fl],
                            axis=-1).astype(BF16)
    gate_col = o_q
    n_ag = o_q + 2 * d
    qkv_tile0 = n_ag // LANES

    row = lambda a: a.reshape(depth, 1, a.shape[-1])
    bf_pad = jnp.pad(b_f, ((0, 0), (0, LANES - n_heads))).reshape(depth, 1, LANES)
    g_mix, g_mlp, g_ple = row(norm_mix), row(norm_mlp), row(norm_ple)
    qg, kg = row(q_gain), row(k_gain)
    cb, br, bi, lam = row(conv_b), row(b_rg), row(b_ig), row(lru_lambda)
    w_r16, w_i16 = w_rg.astype(BF16), w_ig.astype(BF16)
    w_a16, w_b16, w_o16 = w_a_out.astype(BF16), w_b_out.astype(BF16), w_o.astype(BF16)
    w_up16, w_dn16 = w_up.astype(BF16), w_down.astype(BF16)
    w_pe16, w_pg16 = w_pe.astype(BF16), w_pg.astype(BF16)

    n_p, n_s = bp * sp, bs * ts
    pp = p_prompt.reshape(depth, n_p, d_ple)
    ps = p_sample.reshape(depth, n_s, d_ple)

    rows = depth * bs * n_heads
    c_past = _cumsum_lanes(cache_logf.reshape(rows, p_len), tr=_tile(rows, 256))
    c_past = c_past.reshape(depth, bs, n_heads, 1, p_len)

    tn = _tile(d, 1024)
    tf = _tile(w_up.shape[-1], 1024)

    def tokenwise(x, y_a, y_b, z, pe, layer):
        tm = _tile(x.shape[0], 512)
        merged = _merge(y_a, y_b, z, w_a16, w_b16, layer, gate_col=gate_col, tm=tm)
        x = _matmul_res(merged, w_o16, x, layer, tm=tm)
        x = _mlp(x, g_mlp, w_up16, w_dn16, layer, tm=tm, tf=tf)
        return _ple(x, g_ple, w_pg16, pe, w_pe16, layer, tm=tm)

    xp = x_prompt.reshape(n_p, d)
    xs = x_sample.reshape(n_s, d)
    tm_p = _tile(n_p, 1024)
    tm_s = _tile(n_s, 1024)
    tq = _tile(sp, 512)
    conv0 = jnp.zeros((bp, CONV_WIDTH - 1, d_rnn), F32)
    h00 = jnp.zeros((bp, 1, d_rnn), F32)
    outs = [[] for _ in range(10)]
    for layer in range(depth):
        z = _normed_matmul(xp, g_mix, w_all, layer, n_out=n_ag, tm=tm_p, tn=tn)
        qp, kp, vt, k1, v1, lf1 = _qkv_prompt(xp.reshape(bp, sp, d), g_mix, w_all, bf_pad, qg, kg,
                                              layer, n_heads=n_heads, col0=n_ag, tm=tq)
        y_a, c1, l1 = _rglru(z.reshape(bp, sp, -1), conv0, h00, conv_w, cb, w_r16, br, w_i16, bi, lam,
                             layer, tc=_tile(sp, 256))
        y_b = _attn_prompt(qp, kp, vt, tq=_tile(sp, 2 * tq), cw=_tile(tq, 256))
        xp = tokenwise(xp, y_a.reshape(n_p, d_rnn), y_b.reshape(n_p, d_attn), z, pp, layer)
        f1 = lf1[:, :, :n_heads].transpose(0, 2, 1)

        zs = _normed_matmul(xs, g_mix, w_all, layer, n_out=w_all.shape[-1], tm=tm_s, tn=tn)
        zs3 = zs.reshape(bs, ts, -1)
        y_a, c2, l2 = _rglru(zs3, state_conv[layer], state_lru[layer].reshape(bs, 1, d_rnn), conv_w, cb,
                             w_r16, br, w_i16, bi, lam, layer, tc=ts)
        y_b, k2, v2, lf2 = _attn_sample(zs3, bf_pad, qg, kg, cache_k, cache_v, c_past, layer,
                                        n_heads=n_heads, col0=qkv_tile0)
        xs = tokenwise(xs, y_a.reshape(n_s, d_rnn), y_b.reshape(n_s, d_attn), zs, ps, layer)
        f2 = lf2[:, :, :n_heads].transpose(0, 2, 1)

        for lst, val in zip(outs, (k1, v1, f1, c1, l1.reshape(bp, d_rnn),
                                   k2, v2, f2, c2, l2.reshape(bs, d_rnn))):
            lst.append(val)

    return (xp.reshape(bp, sp, d), xs.reshape(bs, ts, d)) + tuple(jnp.stack(o) for o in outs)
```

```python
import functools

import jax
import jax.numpy as jnp
from jax import lax
from jax.experimental import pallas as pl
from jax.experimental.pallas import tpu as pltpu

F32 = jnp.float32
BF16 = jnp.bfloat16

EPS = 1e-6
LRU_C = 8.0
CONV_WIDTH = 4
LANES = 128
SUBLANES = 8
BF16_ROWS = 16
AUG = 128
NEG_BIG = -1e30
LOG2_E = 1.4426950408889634
VMEM_LIMIT = 56 * 1024 * 1024


def _params(*sem):
    return pltpu.CompilerParams(dimension_semantics=sem, vmem_limit_bytes=VMEM_LIMIT)


def _rms(x, g):
    return x * lax.rsqrt(jnp.mean(x * x, axis=-1, keepdims=True) + EPS) * g


def _log_sigmoid(x):
    return jnp.minimum(x, 0.0) - jnp.log1p(jnp.exp(-jnp.abs(x)))


def _sigmoid(x):
    return 1.0 / (1.0 + jnp.exp(-x))


def _split3(c):
    hi = c.astype(BF16).astype(F32)
    r = c - hi
    mid = r.astype(BF16).astype(F32)
    lo = (r - mid).astype(BF16).astype(F32)
    return hi, mid, lo


def _aug_q(c, lane):
    hi, mid, lo = _split3(c)
    return jnp.where(lane == 0, hi, jnp.where(lane == 1, mid, jnp.where(
        lane == 2, lo, jnp.where(lane < 6, 1.0, 0.0))))


def _aug_k(c, lane):
    hi, mid, lo = _split3(c)
    return jnp.where(lane < 3, 1.0, jnp.where(lane == 3, -hi, jnp.where(
        lane == 4, -mid, jnp.where(lane == 5, -lo, 0.0))))


def _cumsum_rows(x):
    n = x.shape[0]
    row = lax.broadcasted_iota(jnp.int32, x.shape, 0)
    d = 1
    while d < n:
        x = x + jnp.where(row >= d, pltpu.roll(x, d, 0), 0.0)
        d *= 2
    return x


def _normed_matmul_kernel(x_ref, g_ref, w_ref, o_ref, xn_ref):
    @pl.when(pl.program_id(1) == 0)
    def _():
        xn_ref[...] = _rms(x_ref[...], g_ref[...]).astype(BF16)

    o_ref[...] = jnp.dot(xn_ref[...], w_ref[...], preferred_element_type=F32)


def _normed_matmul(x, g, w, layer, *, n_out, tm, tn):
    n, d = x.shape
    return pl.pallas_call(
        _normed_matmul_kernel,
        out_shape=jax.ShapeDtypeStruct((n, n_out), F32),
        grid=(n // tm, n_out // tn),
        in_specs=[
            pl.BlockSpec((tm, d), lambda i, j: (i, 0)),
            pl.BlockSpec((None, 1, d), lambda i, j: (layer, 0, 0)),
            pl.BlockSpec((None, d, tn), lambda i, j: (layer, 0, j)),
        ],
        out_specs=pl.BlockSpec((tm, tn), lambda i, j: (i, j)),
        scratch_shapes=[pltpu.VMEM((tm, d), BF16)],
        compiler_params=_params("parallel", "arbitrary"),
        name="normed_matmul",
    )(x, g, w)


def _qkv_prompt_kernel(x_ref, g_ref, wq_ref, wk_ref, wv_ref, wfl_ref, bf_ref, qg_ref, kg_ref,
                       qt_ref, kp_ref, vt_ref, ko_ref, vo_ref, lf_ref, carry_ref, *, n_heads, scale):
    tm = x_ref.shape[0]
    xn = _rms(x_ref[...], g_ref[...]).astype(BF16)
    logf = _log_sigmoid(jnp.dot(xn, wfl_ref[...], preferred_element_type=F32) + bf_ref[...])
    lf_ref[...] = logf

    @pl.when(pl.program_id(1) == 0)
    def _():
        carry_ref[...] = jnp.zeros_like(carry_ref)

    c = _cumsum_rows(logf) + carry_ref[...]
    carry_ref[...] = c[tm - 1:tm, :]
    hi, mid, lo = _split3(c * LOG2_E)
    lane = lax.broadcasted_iota(jnp.int32, (tm, LANES), 1)

    def to_lane(x, dst, h):
        return pltpu.roll(x, (dst - h) % LANES, 1)

    hi_t, mid_t, lo_t = hi.T, mid.T, lo.T
    row = lax.broadcasted_iota(jnp.int32, (BF16_ROWS, tm), 0)
    qg = qg_ref[...] * (scale * LOG2_E)
    zq = jnp.dot(xn, wq_ref[...], preferred_element_type=F32)
    for h in range(n_heads):
        aug = jnp.where(row == 0, hi_t[h:h + 1], jnp.where(row == 1, mid_t[h:h + 1], jnp.where(
            row == 2, lo_t[h:h + 1], jnp.where(row < 6, 1.0, 0.0))))
        qt_ref[h, 0:LANES, :] = _rms(zq[:, h * LANES:(h + 1) * LANES], qg).T.astype(BF16)
        qt_ref[h, LANES:LANES + BF16_ROWS, :] = aug.astype(BF16)
        qt_ref[h, LANES + BF16_ROWS:LANES + AUG, :] = jnp.zeros((AUG - BF16_ROWS, tm), BF16)

    zk = jnp.dot(xn, wk_ref[...], preferred_element_type=F32)
    for h in range(n_heads):
        aug = jnp.where(lane < 3, 1.0, jnp.where(lane == 3, -to_lane(hi, 3, h), jnp.where(
            lane == 4, -to_lane(mid, 4, h), jnp.where(lane == 5, -to_lane(lo, 5, h), 0.0))))
        kn = _rms(zk[:, h * LANES:(h + 1) * LANES], kg_ref[...])
        ko_ref[h] = kn
        kp_ref[h, :, 0:LANES] = kn.astype(BF16)
        kp_ref[h, :, LANES:LANES + AUG] = aug.astype(BF16)

    ones_row = jnp.where(row == 0, 1.0, 0.0).astype(BF16)
    zv = jnp.dot(xn, wv_ref[...], preferred_element_type=F32)
    for h in range(n_heads):
        zh = zv[:, h * LANES:(h + 1) * LANES]
        vo_ref[h] = zh
        vt_ref[h, 0:LANES, :] = zh.T.astype(BF16)
        vt_ref[h, LANES:LANES + BF16_ROWS, :] = ones_row


def _qkv_prompt(x, g, w, b_f, q_gain, k_gain, layer, *, n_heads, col0, tm):
    b, s, d = x.shape
    sec0 = col0 // (n_heads * LANES)
    fl_tile = col0 // LANES + 3 * n_heads
    hd = LANES
    d_attn = n_heads * hd
    scale = float(hd) ** -0.5
    head_major = lambda width, dt: jax.ShapeDtypeStruct((b, n_heads, s, width), dt)
    head_spec = lambda width: pl.BlockSpec((None, n_heads, tm, width), lambda bi, i: (bi, 0, i, 0))
    resident = lambda shape, idx: pl.BlockSpec(shape, idx, pipeline_mode=pl.Buffered(1))
    wsec = lambda sec: resident((None, d, d_attn), lambda bi, i: (layer, 0, sec0 + sec))
    return pl.pallas_call(
        functools.partial(_qkv_prompt_kernel, n_heads=n_heads, scale=scale),
        out_shape=(jax.ShapeDtypeStruct((b, n_heads, hd + AUG, s), BF16), head_major(hd + AUG, BF16),
                   jax.ShapeDtypeStruct((b, n_heads, s // tm, hd + BF16_ROWS, tm), BF16),
                   head_major(hd, F32), head_major(hd, F32),
                   jax.ShapeDtypeStruct((b, s, LANES), F32)),
        grid=(b, s // tm),
        in_specs=[
            pl.BlockSpec((None, tm, d), lambda bi, i: (bi, i, 0)),
            pl.BlockSpec((None, 1, d), lambda bi, i: (layer, 0, 0)),
            wsec(0), wsec(1), wsec(2),
            resident((None, d, LANES), lambda bi, i: (layer, 0, fl_tile)),
            pl.BlockSpec((None, 1, LANES), lambda bi, i: (layer, 0, 0)),
            pl.BlockSpec((None, 1, hd), lambda bi, i: (layer, 0, 0)),
            pl.BlockSpec((None, 1, hd), lambda bi, i: (layer, 0, 0)),
        ],
        out_specs=(pl.BlockSpec((None, n_heads, hd + AUG, tm), lambda bi, i: (bi, 0, 0, i)),
                   head_spec(hd + AUG),
                   pl.BlockSpec((None, n_heads, None, hd + BF16_ROWS, tm), lambda bi, i: (bi, 0, i, 0, 0)),
                   head_spec(hd), head_spec(hd),
                   pl.BlockSpec((None, tm, LANES), lambda bi, i: (bi, i, 0))),
        scratch_shapes=[pltpu.VMEM((1, LANES), F32)],
        compiler_params=_params("parallel", "arbitrary"),
        name="qkv_prompt",
    )(x, g, w, w, w, w, b_f, q_gain, k_gain)


def _rglru_kernel(xa_ref, ga_ref, cprev_ref, h0_ref, cw_ref, cb_ref, wr_ref, br_ref,
                  wi_ref, bi_ref, lam_ref, ya_ref, cnew_ref, hlast_ref,
                  xpad_ref, a_ref, u_ref, hs_ref, hc_ref, *, n_blocks):
    t = pl.program_id(1)
    tc = xa_ref.shape[0]
    pad = SUBLANES

    @pl.when(t == 0)
    def _():
        xpad_ref[pad - (CONV_WIDTH - 1):pad, :] = cprev_ref[...]
        hc_ref[...] = h0_ref[...]

    xa = xa_ref[...]
    xpad_ref[pad:pad + tc, :] = xa
    xc = cb_ref[...] + cw_ref[CONV_WIDTH - 1:CONV_WIDTH, :] * xa
    for jj in range(CONV_WIDTH - 1):
        off = pad - (CONV_WIDTH - 1) + jj
        xc = xc + cw_ref[jj:jj + 1, :] * xpad_ref[off:off + tc, :]
    tail = xa[tc - (CONV_WIDTH - 1):tc, :]
    xpad_ref[pad - (CONV_WIDTH - 1):pad, :] = tail
    cnew_ref[...] = tail

    lam = lam_ref[...]
    softplus_neg_lam = jnp.maximum(-lam, 0.0) + jnp.log1p(jnp.exp(-jnp.abs(lam)))
    for nb in range(n_blocks):
        sl = slice(nb * LANES, (nb + 1) * LANES)
        xb = xc[:, sl]
        xb16 = xb.astype(BF16)
        r = _sigmoid(jnp.dot(xb16, wr_ref[nb], preferred_element_type=F32) + br_ref[:, sl])
        ig = _sigmoid(jnp.dot(xb16, wi_ref[nb], preferred_element_type=F32) + bi_ref[:, sl])
        log_a = (-LRU_C) * r * softplus_neg_lam[:, sl]
        a = jnp.exp(log_a)
        a_ref[:, sl] = a
        u_ref[:, sl] = jnp.sqrt(jnp.tanh(-log_a) * (1.0 + a * a)) * (ig * xb)

    def step(g, h):
        base = pl.multiple_of(g * SUBLANES, SUBLANES)
        for r_ in range(SUBLANES):
            h = a_ref[pl.ds(base + r_, 1), :] * h + u_ref[pl.ds(base + r_, 1), :]
            hs_ref[pl.ds(base + r_, 1), :] = h
        return h

    h_fin = lax.fori_loop(0, tc // SUBLANES, step, hc_ref[...])
    hc_ref[...] = h_fin
    hlast_ref[...] = h_fin
    ya_ref[...] = (hs_ref[...] * jax.nn.gelu(ga_ref[...])).astype(BF16)


def _rglru(z, conv_prev, h0, conv_w, conv_b, w_r, b_r, w_i, b_i, lam, layer, *, tc):
    b, t, _ = z.shape
    d_rnn = conv_w.shape[-1]
    n_blocks = w_r.shape[1]
    vec = lambda: pl.BlockSpec((None, 1, d_rnn), lambda bi, ti: (layer, 0, 0))
    gate_w = lambda: pl.BlockSpec((None, n_blocks, LANES, LANES), lambda bi, ti: (layer, 0, 0, 0))
    return pl.pallas_call(
        functools.partial(_rglru_kernel, n_blocks=n_blocks),
        out_shape=(jax.ShapeDtypeStruct((b, t, d_rnn), BF16),
                   jax.ShapeDtypeStruct((b, CONV_WIDTH - 1, d_rnn), F32),
                   jax.ShapeDtypeStruct((b, 1, d_rnn), F32)),
        grid=(b, t // tc),
        in_specs=[
            pl.BlockSpec((None, tc, d_rnn), lambda bi, ti: (bi, ti, 0)),
            pl.BlockSpec((None, tc, d_rnn), lambda bi, ti: (bi, ti, 1)),
            pl.BlockSpec((None, CONV_WIDTH - 1, d_rnn), lambda bi, ti: (bi, 0, 0)),
            pl.BlockSpec((None, 1, d_rnn), lambda bi, ti: (bi, 0, 0)),
            pl.BlockSpec((None, CONV_WIDTH, d_rnn), lambda bi, ti: (layer, 0, 0)),
            vec(), gate_w(), vec(), gate_w(), vec(), vec(),
        ],
        out_specs=(pl.BlockSpec((None, tc, d_rnn), lambda bi, ti: (bi, ti, 0)),
                   pl.BlockSpec((None, CONV_WIDTH - 1, d_rnn), lambda bi, ti: (bi, 0, 0)),
                   pl.BlockSpec((None, 1, d_rnn), lambda bi, ti: (bi, 0, 0))),
        scratch_shapes=[pltpu.VMEM((tc + SUBLANES, d_rnn), F32), pltpu.VMEM((tc, d_rnn), F32),
                        pltpu.VMEM((tc, d_rnn), F32), pltpu.VMEM((tc, d_rnn), F32),
                        pltpu.VMEM((1, d_rnn), F32)],
        compiler_params=_params("parallel", "arbitrary"),
        name="rglru",
    )(z, z, conv_prev, h0, conv_w, conv_b, w_r, b_r, w_i, b_i, lam)


def _attn_prompt_kernel(q_ref, k_ref, vt_ref, o_ref, s0_ref, s1_ref, m_ref, acc_ref, *, cw):
    qi = pl.program_id(2)
    tq = q_ref.shape[-1]
    tk = vt_ref.shape[-1]
    hd = o_ref.shape[-1]
    groups = range(tq // cw)

    m_ref[...] = jnp.full(m_ref.shape, NEG_BIG, F32)
    acc_ref[...] = jnp.zeros(acc_ref.shape, F32)

    def qk(jb, s_ref, grps):
        k = k_ref[pl.ds(pl.multiple_of(jb * tk, tk), tk), :]
        for c in grps:
            sl = slice(c * cw, (c + 1) * cw)
            s_ref[:, sl] = jnp.dot(k, q_ref[:, sl], preferred_element_type=F32)

    def softmax_pv(jb, s_ref, grps, k_lo):
        vt = vt_ref[jb]
        for c in grps:
            sl = slice(c * cw, (c + 1) * cw)
            s = s_ref[:, sl]
            if k_lo is not None and k_lo + tk - 1 > c * cw:
                key = lax.broadcasted_iota(jnp.int32, (tk, cw), 0) + (k_lo - c * cw)
                qry = lax.broadcasted_iota(jnp.int32, (tk, cw), 1)
                s = jnp.where(key <= qry, s, NEG_BIG)
            m_prev = m_ref[:, sl]
            m_new = jnp.maximum(m_prev, jnp.max(s, axis=0, keepdims=True))
            alpha = jnp.exp2(m_prev - m_new)
            p = jnp.exp2(s - m_new)
            acc_ref[:, sl] = alpha * acc_ref[:, sl] + jnp.dot(vt, p.astype(BF16),
                                                              preferred_element_type=F32)
            m_ref[:, sl] = m_new

    qk(0, s0_ref, groups)

    def body(i, carry):
        qk(2 * i + 1, s1_ref, groups)
        softmax_pv(2 * i, s0_ref, groups, None)
        qk(2 * i + 2, s0_ref, groups)
        softmax_pv(2 * i + 1, s1_ref, groups, None)
        return carry

    lax.fori_loop(0, qi, body, 0)

    late = [c for c in groups if c * cw + cw - 1 >= tk]
    qk(2 * qi + 1, s1_ref, late)
    softmax_pv(2 * qi, s0_ref, groups, 0)
    softmax_pv(2 * qi + 1, s1_ref, late, tk)

    o_ref[...] = (acc_ref[0:hd, :] / acc_ref[hd:hd + 1, :]).T.astype(o_ref.dtype)


def _attn_prompt(qt, kp, vt, *, tq, cw):
    b, h, s, dq = kp.shape
    _, _, nblk, hdp, tk = vt.shape
    hd = hdp - BF16_ROWS
    assert tq == 2 * tk and tk % cw == 0
    return pl.pallas_call(
        functools.partial(_attn_prompt_kernel, cw=cw),
        out_shape=jax.ShapeDtypeStruct((b, s, h * hd), BF16),
        grid=(b, h, s // tq),
        in_specs=[
            pl.BlockSpec((None, None, dq, tq), lambda bi, hi, qi: (bi, hi, 0, qi)),
            pl.BlockSpec((None, None, s, dq), lambda bi, hi, qi: (bi, hi, 0, 0)),
            pl.BlockSpec((None, None, nblk, hdp, tk), lambda bi, hi, qi: (bi, hi, 0, 0, 0)),
        ],
        out_specs=pl.BlockSpec((None, tq, hd), lambda bi, hi, qi: (bi, qi, hi)),
        scratch_shapes=[pltpu.VMEM((tk, tq), F32), pltpu.VMEM((tk, tq), F32),
                        pltpu.VMEM((1, tq), F32), pltpu.VMEM((hdp, tq), F32)],
        compiler_params=_params("parallel", "parallel", "arbitrary"),
        name="attn_prompt",
    )(qt, kp, vt)


def _cumsum_lanes_kernel(x_ref, o_ref):
    rows, n = x_ref.shape
    r_i = lax.broadcasted_iota(jnp.int32, (LANES, LANES), 0)
    c_i = lax.broadcasted_iota(jnp.int32, (LANES, LANES), 1)
    upper = jnp.where(r_i <= c_i, 1.0, 0.0).astype(BF16)
    carry = jnp.zeros((rows, 1), F32)
    for ch in range(n // LANES):
        sl = slice(ch * LANES, (ch + 1) * LANES)
        cs = carry
        for piece in _split3(x_ref[:, sl]):
            cs = cs + jnp.dot(piece.astype(BF16), upper, preferred_element_type=F32)
        o_ref[:, sl] = cs
        carry = cs[:, LANES - 1:LANES]


def _cumsum_lanes(x, *, tr):
    rows, n = x.shape
    return pl.pallas_call(
        _cumsum_lanes_kernel,
        out_shape=jax.ShapeDtypeStruct((rows, n), F32),
        grid=(rows // tr,),
        in_specs=[pl.BlockSpec((tr, n), lambda i: (i, 0))],
        out_specs=pl.BlockSpec((tr, n), lambda i: (i, 0)),
        compiler_params=_params("parallel"),
        name="cumsum_lanes",
    )(x)


def _attn_sample_kernel(zq_ref, zk_ref, zv_ref, fl_ref, bf_ref, qg_ref, kg_ref,
                        kc_ref, vc_ref, cp_ref, yb_ref, ko_ref, vo_ref, lf_ref, *, scale):
    n_heads, p_len, _ = kc_ref.shape
    t = zq_ref.shape[0]
    lane = lax.broadcasted_iota(jnp.int32, (t, LANES), 1)
    row = lax.broadcasted_iota(jnp.int32, (t, LANES), 0)

    logf_all = _log_sigmoid(fl_ref[...] + bf_ref[...])
    lf_ref[...] = logf_all

    for h in range(n_heads):
        sl = slice(h * LANES, (h + 1) * LANES)
        cl = _cumsum_rows(jnp.broadcast_to(logf_all[:, h:h + 1], (t, LANES)))
        qn = _rms(zq_ref[:, sl], qg_ref[...]) * scale
        kn = _rms(zk_ref[:, sl], kg_ref[...])
        v = zv_ref[:, sl]
        ko_ref[h] = kn
        vo_ref[h] = v

        cp = cp_ref[h]
        s_p = lax.dot_general(qn.astype(BF16), kc_ref[h].astype(BF16), (((1,), (1,)), ((), ())),
                              preferred_element_type=F32)
        s_p = s_p + cl[:, 0:1] + (cp[:, p_len - 1:p_len] - cp)

        q_aug = jnp.concatenate([qn, _aug_q(cl, lane)], axis=1).astype(BF16)
        k_aug = jnp.concatenate([kn, _aug_k(cl, lane)], axis=1).astype(BF16)
        k_pad = jnp.concatenate([k_aug, jnp.zeros((LANES - t, 2 * LANES), BF16)], axis=0)
        v_pad = jnp.concatenate([v.astype(BF16), jnp.zeros((LANES - t, LANES), BF16)], axis=0)
        s_n = lax.dot_general(q_aug, k_pad, (((1,), (1,)), ((), ())), preferred_element_type=F32)
        s_n = jnp.where(lane <= row, s_n, NEG_BIG)

        m = jnp.maximum(jnp.max(s_p, axis=1, keepdims=True), jnp.max(s_n, axis=1, keepdims=True))
        p_p = jnp.exp(s_p - m)
        p_n = jnp.exp(s_n - m)
        l = jnp.sum(p_p, axis=1, keepdims=True) + jnp.sum(p_n, axis=1, keepdims=True)
        o = jnp.dot(p_p.astype(BF16), vc_ref[h].astype(BF16), preferred_element_type=F32)
        o = o + jnp.dot(p_n.astype(BF16), v_pad, preferred_element_type=F32)
        yb_ref[:, sl] = (o / l).astype(yb_ref.dtype)


def _attn_sample(z, b_f, q_gain, k_gain, cache_k, cache_v, c_past, layer, *, n_heads, col0):
    b, t, _ = z.shape
    hd = LANES
    p_len = cache_k.shape[-2]
    scale = float(hd) ** -0.5
    d_attn = n_heads * hd
    sec0 = col0 // n_heads
    zspec = lambda sec: pl.BlockSpec((None, t, d_attn), lambda bi: (bi, 0, sec0 + sec))
    cache = lambda: pl.BlockSpec((None, None, n_heads, p_len, hd), lambda bi: (layer, bi, 0, 0, 0))
    heads = lambda: pl.BlockSpec((None, n_heads, t, hd), lambda bi: (bi, 0, 0, 0))
    return pl.pallas_call(
        functools.partial(_attn_sample_kernel, scale=scale),
        out_shape=(jax.ShapeDtypeStruct((b, t, d_attn), BF16),
                   jax.ShapeDtypeStruct((b, n_heads, t, hd), F32),
                   jax.ShapeDtypeStruct((b, n_heads, t, hd), F32),
                   jax.ShapeDtypeStruct((b, t, LANES), F32)),
        grid=(b,),
        in_specs=[
            zspec(0), zspec(1), zspec(2),
            pl.BlockSpec((None, t, LANES), lambda bi: (bi, 0, col0 + 3 * n_heads)),
            pl.BlockSpec((None, 1, LANES), lambda bi: (layer, 0, 0)),
            pl.BlockSpec((None, 1, hd), lambda bi: (layer, 0, 0)),
            pl.BlockSpec((None, 1, hd), lambda bi: (layer, 0, 0)),
            cache(), cache(),
            pl.BlockSpec((None, None, n_heads, 1, p_len), lambda bi: (layer, bi, 0, 0, 0)),
        ],
        out_specs=(pl.BlockSpec((None, t, d_attn), lambda bi: (bi, 0, 0)),
                   heads(), heads(),
                   pl.BlockSpec((None, t, LANES), lambda bi: (bi, 0, 0))),
        compiler_params=_params("parallel"),
        name="attn_sample",
    )(z, z, z, z, b_f, q_gain, k_gain, cache_k, cache_v, c_past)


def _merge_kernel(ya_ref, yb_ref, ga_ref, gb_ref, wa_ref, wb_ref, o_ref, *, cn):
    ya, yb = ya_ref[...], yb_ref[...]
    for j0 in range(0, o_ref.shape[-1], cn):
        sl = slice(j0, j0 + cn)
        a = jnp.dot(ya, wa_ref[:, sl], preferred_element_type=F32)
        b = jnp.dot(yb, wb_ref[:, sl], preferred_element_type=F32)
        o_ref[:, sl] = (_sigmoid(ga_ref[:, sl]) * a + _sigmoid(gb_ref[:, sl]) * b).astype(o_ref.dtype)


def _resident(shape, index_map):
    return pl.BlockSpec(shape, index_map, pipeline_mode=pl.Buffered(1))


def _merge(ya, yb, z, w_a, w_b, layer, *, gate_col, tm):
    n, d_a = ya.shape
    d_b = yb.shape[-1]
    d = w_a.shape[-1]
    ga0 = gate_col // d
    return pl.pallas_call(
        functools.partial(_merge_kernel, cn=_tile(d, 1024)),
        out_shape=jax.ShapeDtypeStruct((n, d), BF16),
        grid=(n // tm,),
        in_specs=[
            pl.BlockSpec((tm, d_a), lambda i: (i, 0)),
            pl.BlockSpec((tm, d_b), lambda i: (i, 0)),
            pl.BlockSpec((tm, d), lambda i: (i, ga0)),
            pl.BlockSpec((tm, d), lambda i: (i, ga0 + 1)),
            _resident((None, d_a, d), lambda i: (layer, 0, 0)),
            _resident((None, d_b, d), lambda i: (layer, 0, 0)),
        ],
        out_specs=pl.BlockSpec((tm, d), lambda i: (i, 0)),
        compiler_params=_params("parallel"),
        name="merge",
    )(ya, yb, z, z, w_a, w_b)


def _matmul_res_kernel(m_ref, w_ref, x_ref, o_ref):
    o_ref[...] = x_ref[...] + jnp.dot(m_ref[...], w_ref[...], preferred_element_type=F32)


def _matmul_res(m, w, x, layer, *, tm):
    n, k = m.shape
    d = w.shape[-1]
    return pl.pallas_call(
        _matmul_res_kernel,
        out_shape=jax.ShapeDtypeStruct((n, d), F32),
        grid=(n // tm,),
        in_specs=[
            pl.BlockSpec((tm, k), lambda i: (i, 0)),
            _resident((None, k, d), lambda i: (layer, 0, 0)),
            pl.BlockSpec((tm, d), lambda i: (i, 0)),
        ],
        out_specs=pl.BlockSpec((tm, d), lambda i: (i, 0)),
        compiler_params=_params("parallel"),
        name="matmul_res",
    )(m, w, x)


def _mlp_kernel(x_ref, g_ref, wu_ref, wd_ref, o_ref, xn_ref):
    @pl.when(pl.program_id(1) == 0)
    def _():
        x = x_ref[...]
        xn_ref[...] = _rms(x, g_ref[...]).astype(BF16)
        o_ref[...] = x

    hid = jnp.maximum(jnp.dot(xn_ref[...], wu_ref[...], preferred_element_type=F32), 0.0)
    o_ref[...] += jnp.dot((hid * hid).astype(BF16), wd_ref[...], preferred_element_type=F32)


def _mlp(x, g, w_up, w_down, layer, *, tm, tf):
    n, d = x.shape
    d_ff = w_up.shape[-1]
    return pl.pallas_call(
        _mlp_kernel,
        out_shape=jax.ShapeDtypeStruct((n, d), F32),
        grid=(n // tm, d_ff // tf),
        in_specs=[
            pl.BlockSpec((tm, d), lambda i, f: (i, 0)),
            pl.BlockSpec((None, 1, d), lambda i, f: (layer, 0, 0)),
            pl.BlockSpec((None, d, tf), lambda i, f: (layer, 0, f)),
            pl.BlockSpec((None, tf, d), lambda i, f: (layer, f, 0)),
        ],
        out_specs=pl.BlockSpec((tm, d), lambda i, f: (i, 0)),
        scratch_shapes=[pltpu.VMEM((tm, d), BF16)],
        compiler_params=_params("parallel", "arbitrary"),
        name="mlp",
    )(x, g, w_up, w_down)


def _ple_kernel(x_ref, g_ref, wg_ref, p_ref, we_ref, o_ref, *, cn):
    xn = _rms(x_ref[...], g_ref[...]).astype(BF16)
    pe = p_ref[...].astype(BF16)
    for j0 in range(0, o_ref.shape[-1], cn):
        sl = slice(j0, j0 + cn)
        gate = _sigmoid(jnp.dot(xn, wg_ref[:, sl], preferred_element_type=F32))
        emb = jnp.dot(pe, we_ref[:, sl], preferred_element_type=F32)
        o_ref[:, sl] = x_ref[:, sl] + gate * emb


def _ple(x, g, w_pg, p, w_pe, layer, *, tm):
    n, d = x.shape
    d_ple = p.shape[-1]
    return pl.pallas_call(
        functools.partial(_ple_kernel, cn=_tile(d, 1024)),
        out_shape=jax.ShapeDtypeStruct((n, d), F32),
        grid=(n // tm,),
        in_specs=[
            pl.BlockSpec((tm, d), lambda i: (i, 0)),
            pl.BlockSpec((None, 1, d), lambda i: (layer, 0, 0)),
            _resident((None, d, d), lambda i: (layer, 0, 0)),
            pl.BlockSpec((None, tm, d_ple), lambda i: (layer, i, 0)),
            _resident((None, d_ple, d), lambda i: (layer, 0, 0)),
        ],
        out_specs=pl.BlockSpec((tm, d), lambda i: (i, 0)),
        compiler_params=_params("parallel"),
        name="ple",
    )(x, g, w_pg, p, w_pe)


def _tile(n, pref):
    t = min(n, pref)
    while n % t:
        t //= 2
    return t


def kernel(x_prompt, x_sample, cache_k, cache_v, cache_logf, state_conv, state_lru, p_prompt, p_sample, norm_mix, norm_mlp, norm_ple, w_in, b_f, conv_w, conv_b, w_rg, b_rg, w_ig, b_ig, lru_lambda, q_gain, k_gain, w_a_out, w_b_out, w_o, w_up, w_down, w_pe, w_pg):
    depth, d, _ = w_in.shape
    bp, sp, _ = x_prompt.shape
    bs, ts, _ = x_sample.shape
    d_rnn = conv_w.shape[-1]
    n_heads = b_f.shape[-1]
    hd = q_gain.shape[-1]
    d_attn = n_heads * hd
    d_ple = w_pe.shape[1]
    p_len = cache_k.shape[-2]
    assert hd == LANES and w_rg.shape[-1] == LANES and n_heads <= LANES

    o_q = 2 * d_rnn
    o_fl = o_q + 3 * d_attn
    o_gate = o_fl + n_heads
    fl_pad = _tile(d_attn, 1024)
    w_fl = lax.optimization_barrier(w_in[:, :, o_fl:o_gate])
    w_fl = jnp.pad(w_fl, ((0, 0), (0, 0), (0, fl_pad - n_heads)))
    w_all = jnp.concatenate([w_in[:, :, :o_q], w_in[:, :, o_gate:], w_in[:, :, o_q:o_fl], w_fl],
                            axis=-1).astype(BF16)
    gate_col = o_q
    n_ag = o_q + 2 * d
    qkv_tile0 = n_ag // LANES

    row = lambda a: a.reshape(depth, 1, a.shape[-1])
    bf_pad = jnp.pad(b_f, ((0, 0), (0, LANES - n_heads))).reshape(depth, 1, LANES)
    g_mix, g_mlp, g_ple = row(norm_mix), row(norm_mlp), row(norm_ple)
    qg, kg = row(q_gain), row(k_gain)
    cb, br, bi, lam = row(conv_b), row(b_rg), row(b_ig), row(lru_lambda)
    w_r16, w_i16 = w_rg.astype(BF16), w_ig.astype(BF16)
    w_a16, w_b16, w_o16 = w_a_out.astype(BF16), w_b_out.astype(BF16), w_o.astype(BF16)
    w_up16, w_dn16 = w_up.astype(BF16), w_down.astype(BF16)
    w_pe16, w_pg16 = w_pe.astype(BF16), w_pg.astype(BF16)

    n_p, n_s = bp * sp, bs * ts
    pp = p_prompt.reshape(depth, n_p, d_ple)
    ps = p_sample.reshape(depth, n_s, d_ple)

    rows = depth * bs * n_heads
    c_past = _cumsum_lanes(cache_logf.reshape(rows, p_len), tr=_tile(rows, 256))
    c_past = c_past.reshape(depth, bs, n_heads, 1, p_len)

    tn = _tile(d, 1024)
    tf = _tile(w_up.shape[-1], 1024)

    def tokenwise(x, y_a, y_b, z, pe, layer):
        tm = _tile(x.shape[0], 512)
        merged = _merge(y_a, y_b, z, w_a16, w_b16, layer, gate_col=gate_col, tm=tm)
        x = _matmul_res(merged, w_o16, x, layer, tm=tm)
        x = _mlp(x, g_mlp, w_up16, w_dn16, layer, tm=tm, tf=tf)
        return _ple(x, g_ple, w_pg16, pe, w_pe16, layer, tm=tm)

    xp = x_prompt.reshape(n_p, d)
    xs = x_sample.reshape(n_s, d)
    tm_p = _tile(n_p, 1024)
    tm_s = _tile(n_s, 1024)
    tq = _tile(sp, 512)
    conv0 = jnp.zeros((bp, CONV_WIDTH - 1, d_rnn), F32)
    h00 = jnp.zeros((bp, 1, d_rnn), F32)
    outs = [[] for _ in range(10)]
    for layer in range(depth):
        z = _normed_matmul(xp, g_mix, w_all, layer, n_out=n_ag, tm=tm_p, tn=tn)
        qp, kp, vt, k1, v1, lf1 = _qkv_prompt(xp.reshape(bp, sp, d), g_mix, w_all, bf_pad, qg, kg,
                                              layer, n_heads=n_heads, col0=n_ag, tm=tq)
        y_a, c1, l1 = _rglru(z.reshape(bp, sp, -1), conv0, h00, conv_w, cb, w_r16, br, w_i16, bi, lam,
                             layer, tc=_tile(sp, 256))
        y_b = _attn_prompt(qp, kp, vt, tq=_tile(sp, 2 * tq), cw=_tile(tq, 512))
        xp = tokenwise(xp, y_a.reshape(n_p, d_rnn), y_b.reshape(n_p, d_attn), z, pp, layer)
        f1 = lf1[:, :, :n_heads].transpose(0, 2, 1)

        zs = _normed_matmul(xs, g_mix, w_all, layer, n_out=w_all.shape[-1], tm=tm_s, tn=tn)
        zs3 = zs.reshape(bs, ts, -1)
        y_a, c2, l2 = _rglru(zs3, state_conv[layer], state_lru[layer].reshape(bs, 1, d_rnn), conv_w, cb,
                             w_r16, br, w_i16, bi, lam, layer, tc=ts)
        y_b, k2, v2, lf2 = _attn_sample(zs3, bf_pad, qg, kg, cache_k, cache_v, c_past, layer,
                                        n_heads=n_heads, col0=qkv_tile0)
        xs = tokenwise(xs, y_a.reshape(n_s, d_rnn), y_b.reshape(n_s, d_attn), zs, ps, layer)
        f2 = lf2[:, :, :n_heads].transpose(0, 2, 1)

        for lst, val in zip(outs, (k1, v1, f1, c1, l1.reshape(bp, d_rnn),
                                   k2, v2, f2, c2, l2.reshape(bs, d_rnn))):
            lst.append(val)

    return (xp.reshape(bp, sp, d), xs.reshape(bs, ts, d)) + tuple(jnp.stack(o) for o in outs)
```
